```python
import math
import jax, jax.numpy as jnp
from jax import lax
import numpy as np

D_MODEL = 4096
BATCH = 4
SEQ = 2048
DEPTH = 1
DEC_BATCH = 32
DEC_SEQ = 4
PAST_LEN = 8192
PAGE_SIZE = 128

D_RWKV = D_MODEL // 2
RW_HEAD = 64
RW_HEADS = D_RWKV // RW_HEAD
DECAY_LORA = 96
AAA_LORA = 96
GATE_LORA = 256
RW_COLS = 3 * D_RWKV + DECAY_LORA + AAA_LORA + GATE_LORA
GN_EPS = 64e-5
D_ATT = D_MODEL // 2
ATT_HEADS = 8
ATT_QK = D_ATT // (2 * ATT_HEADS)
ATT_V = 2 * ATT_QK
Q_BLOCK = 128
MEM_TOKENS = 256
MEM_HEADS = 4
MEM_HEAD = 128
D_MEM = MEM_HEADS * MEM_HEAD
N_BRANCH = 3
IN_COLS = RW_COLS + 3 * D_ATT + D_MEM + N_BRANCH * D_MODEL
D_FF = 4 * D_MODEL
NORM_EPS = 1e-6
NEG_INF = -1e30

kernel_name = 'rwkv7_diffattn_alibi_memory_hybrid_step'


def rmsnorm(x, g):
    xf = x.astype(jnp.float32)
    y = xf * lax.rsqrt(jnp.mean(xf * xf, axis=-1, keepdims=True) + NORM_EPS)
    return (y * g.astype(jnp.float32)).astype(x.dtype)


def alibi_slopes():
    return jnp.asarray([2.0 ** (-8.0 * (h + 1) / ATT_HEADS) for h in range(ATT_HEADS)], dtype=jnp.float32)


def rwkv7_branch(p, shift_prev, s0, mu_shift, w0, w_lora_up, a0, a_lora_up, g_lora_up,
                 k_k, k_a, r_k, ln_x_w, ln_x_b):
    f32 = jnp.float32
    B, T, _ = p.shape
    prev = jnp.concatenate([shift_prev[:, None].astype(p.dtype), p[:, :-1]], axis=1)
    xs = p + (prev - p) * mu_shift
    r, k, v, wd, ad, gd = jnp.split(
        xs, [D_RWKV, 2 * D_RWKV, 3 * D_RWKV, 3 * D_RWKV + DECAY_LORA,
             3 * D_RWKV + DECAY_LORA + AAA_LORA], axis=-1)
    w = -jax.nn.softplus(-(w0 + jnp.tanh(wd) @ w_lora_up).astype(f32)) - 0.5
    decay = jnp.exp(-jnp.exp(w))
    a = jax.nn.sigmoid((a0 + ad @ a_lora_up).astype(f32))
    g = jax.nn.sigmoid(gd) @ g_lora_up
    hd = lambda t: t.astype(f32).reshape(B, T, RW_HEADS, RW_HEAD)
    kk = hd(k * k_k)
    kk = kk / jnp.maximum(jnp.sqrt(jnp.sum(kk * kk, axis=-1, keepdims=True)), 1e-12)
    k_mod = k.astype(f32) * (1.0 + (a - 1.0) * k_a.astype(f32))
    rh, kh, vh, ah, wh = hd(r), hd(k_mod), hd(v), hd(a), hd(decay)
    tm = lambda t: jnp.moveaxis(t, 1, 0)

    def step(s, inp):
        w_t, r_t, k_t, v_t, kk_t, b_t = inp
        s_kk = jnp.einsum('bhvk,bhk->bhv', s, kk_t)
        s = (s * w_t[:, :, None, :] - s_kk[..., None] * b_t[:, :, None, :]
             + v_t[..., None] * k_t[:, :, None, :])
        return s, jnp.einsum('bhvk,bhk->bhv', s, r_t)

    s_fin, y = lax.scan(step, s0.astype(f32),
                        (tm(wh), tm(rh), tm(kh), tm(vh), tm(kk), tm(kk * ah)))
    y = tm(y)
    mu = jnp.mean(y, axis=-1, keepdims=True)
    var = jnp.mean(jnp.square(y - mu), axis=-1, keepdims=True)
    yn = ((y - mu) * lax.rsqrt(var + GN_EPS)).reshape(B, T, D_RWKV)
    yn = yn * ln_x_w.astype(f32) + ln_x_b.astype(f32)
    bonus = (jnp.sum(rh * kh * r_k.astype(f32), axis=-1, keepdims=True) * vh).reshape(B, T, D_RWKV)
    out = (yn + bonus) * g.astype(f32)
    return out.astype(p.dtype), s_fin, p[:, -1]


def diff_core(q, k, v, q_pos, k_pos, lam, slopes):
    s = jnp.einsum('bqhcd,blhcd->bhcql', q, k).astype(jnp.float32) * (ATT_QK ** -0.5)
    dist = (q_pos[:, None] - k_pos[None, :]).astype(jnp.float32)
    bias = jnp.where(dist >= 0, -slopes[:, None, None] * dist, NEG_INF)
    p = jax.nn.softmax(s + bias[None, :, None], axis=-1)
    attn = p[:, :, 0] - lam * p[:, :, 1]
    return jnp.einsum('bhql,blhe->bqhe', attn.astype(v.dtype), v)


def diff_attn_prompt(q, k, v, lam, slopes):
    B, T = q.shape[:2]
    nb = T // Q_BLOCK
    qb = jnp.swapaxes(q.reshape(B, nb, Q_BLOCK, ATT_HEADS, 2, ATT_QK), 0, 1)
    k_pos = jnp.arange(T)

    def block(args):
        q_i, i = args
        return diff_core(q_i, k, v, i * Q_BLOCK + jnp.arange(Q_BLOCK), k_pos, lam, slopes)

    o = lax.map(block, (qb, jnp.arange(nb)))
    return jnp.swapaxes(o, 0, 1).reshape(B, T, ATT_HEADS, ATT_V)


def diff_attn_sample(q, k_new, v_new, cache_k, cache_v, page_table, lam, slopes):
    DS = q.shape[1]
    past = page_table.shape[1] * PAGE_SIZE
    k_pos = jnp.arange(past + DS)
    q_pos = past + jnp.arange(DS)

    def one(args):
        q_b, kn_b, vn_b, pt_b = args
        kp = cache_k[pt_b].reshape(past, ATT_HEADS, 2, ATT_QK).astype(kn_b.dtype)
        vp = cache_v[pt_b].reshape(past, ATT_HEADS, ATT_V).astype(vn_b.dtype)
        kk = jnp.concatenate([kp, kn_b], axis=0)
        vv = jnp.concatenate([vp, vn_b], axis=0)
        return diff_core(q_b[None], kk[None], vv[None], q_pos, k_pos, lam, slopes)[0]

    return lax.map(one, (q, k_new, v_new, page_table))


def memory_kv(mem, g_mem, w_mem_kv):
    B, M, _ = mem.shape
    kv = rmsnorm(mem, g_mem) @ w_mem_kv
    mk, mv = jnp.split(kv, 2, axis=-1)
    return mk.reshape(B, M, MEM_HEADS, MEM_HEAD), mv.reshape(B, M, MEM_HEADS, MEM_HEAD)


def memory_attn(qm, mk, mv):
    B, T = qm.shape[:2]
    s = jnp.einsum('bthe,bmhe->bhtm', qm, mk).astype(jnp.float32) * (MEM_HEAD ** -0.5)
    p = jax.nn.softmax(s, axis=-1)
    return jnp.einsum('bhtm,bmhe->bthe', p.astype(mv.dtype), mv).reshape(B, T, D_MEM)


def layer(x, shift_prev, s0, mem_k, mem_v, attend, lam_init, W):
    B, T, _ = x.shape
    h = rmsnorm(x, W['g_pre_mix'])
    p = h @ W['w_in']
    c1 = RW_COLS
    c2 = c1 + D_ATT
    c3 = c2 + D_ATT
    c4 = c3 + D_ATT
    c5 = c4 + D_MEM
    p_rw, q, k, v, qm, gates = jnp.split(p, [c1, c2, c3, c4, c5], axis=-1)
    o_rw, s_fin, shift_last = rwkv7_branch(
        p_rw, shift_prev, s0, W['mu_shift'], W['w0'], W['w_lora_up'], W['a0'], W['a_lora_up'],
        W['g_lora_up'], W['k_k'], W['k_a'], W['r_k'], W['ln_x_w'], W['ln_x_b'])
    q = q.reshape(B, T, ATT_HEADS, 2, ATT_QK)
    k = k.reshape(B, T, ATT_HEADS, 2, ATT_QK)
    v = v.reshape(B, T, ATT_HEADS, ATT_V)
    f32 = jnp.float32
    lam = (jnp.exp(jnp.sum(W['lambda_q1'].astype(f32) * W['lambda_k1'].astype(f32)))
           - jnp.exp(jnp.sum(W['lambda_q2'].astype(f32) * W['lambda_k2'].astype(f32))) + lam_init)
    o = attend(q, k, v, lam)
    o_att = (rmsnorm(o, W['g_subln']) * (1.0 - lam_init)).reshape(B, T, D_ATT)
    o_mem = memory_attn(qm.reshape(B, T, MEM_HEADS, MEM_HEAD), mem_k, mem_v)
    g_rw, g_att, g_mem = jnp.split(jax.nn.sigmoid(gates), N_BRANCH, axis=-1)
    m = (g_rw * (o_rw @ W['w_br_rwkv']) + g_att * (o_att @ W['w_br_attn'])
         + g_mem * (o_mem @ W['w_br_mem']))
    x = x + rmsnorm(m @ W['w_out'], W['g_post_mix'])
    u = jnp.square(jax.nn.relu(rmsnorm(x, W['g_pre_mlp']) @ W['w_up']))
    x = x + rmsnorm(u @ W['w_down'], W['g_post_mlp'])
    return x, k.reshape(B, T, ATT_HEADS, 2 * ATT_QK), v, s_fin, shift_last


def setup_inputs(seed: int = 0) -> dict:
    key = jax.random.key(seed)
    ks = iter(jax.random.split(key, 48))
    f32 = jnp.float32
    nrm = lambda shape, scale: scale * jax.random.normal(next(ks), shape, f32)
    gain = lambda n: 1.0 + nrm((n,), 0.05)
    n_pages = PAST_LEN // PAGE_SIZE
    n_used = DEC_BATCH * n_pages
    n_phys = n_used + (n_used + 3) // 4
    x_prompt = nrm((BATCH, SEQ, D_MODEL), 1.0)
    x_sample = nrm((DEC_BATCH, DEC_SEQ, D_MODEL), 1.0)
    cache_k = nrm((n_phys, PAGE_SIZE, ATT_HEADS, 2 * ATT_QK), 1.0)
    cache_v = nrm((n_phys, PAGE_SIZE, ATT_HEADS, ATT_V), 1.0)
    page_table = jax.random.permutation(next(ks), n_phys)[:n_used].reshape(DEC_BATCH, n_pages).astype(jnp.int32)
    state_wkv = nrm((DEC_BATCH, RW_HEADS, RW_HEAD, RW_HEAD), 0.5)
    state_shift = nrm((DEC_BATCH, RW_COLS), 1.0)
    cache_mem_k = nrm((DEC_BATCH, MEM_TOKENS, MEM_HEADS, MEM_HEAD), 1.0)
    cache_mem_v = nrm((DEC_BATCH, MEM_TOKENS, MEM_HEADS, MEM_HEAD), 1.0)
    mem_prompt = nrm((BATCH, MEM_TOKENS, D_MODEL), 1.0)
    return {
        'x_prompt': x_prompt,
        'x_sample': x_sample,
        'cache_k': cache_k,
        'cache_v': cache_v,
        'page_table': page_table,
        'state_wkv': state_wkv,
        'state_shift': state_shift,
        'cache_mem_k': cache_mem_k,
        'cache_mem_v': cache_mem_v,
        'mem_prompt': mem_prompt,
        'g_pre_mix': gain(D_MODEL),
        'w_in': nrm((D_MODEL, IN_COLS), D_MODEL ** -0.5),
        'mu_shift': jax.random.uniform(next(ks), (RW_COLS,), f32),
        'w0': jax.random.uniform(next(ks), (D_RWKV,), f32, minval=-6.0, maxval=1.0),
        'w_lora_up': nrm((DECAY_LORA, D_RWKV), DECAY_LORA ** -0.5),
        'a0': nrm((D_RWKV,), 0.1),
        'a_lora_up': nrm((AAA_LORA, D_RWKV), AAA_LORA ** -0.5),
        'g_lora_up': nrm((GATE_LORA, D_RWKV), GATE_LORA ** -0.5),
        'k_k': 0.85 + nrm((D_RWKV,), 0.05),
        'k_a': 1.0 + nrm((D_RWKV,), 0.05),
        'r_k': nrm((RW_HEADS, RW_HEAD), 0.1),
        'ln_x_w': gain(D_RWKV),
        'ln_x_b': nrm((D_RWKV,), 0.02),
        'lambda_q1': nrm((ATT_QK,), 0.1),
        'lambda_k1': nrm((ATT_QK,), 0.1),
        'lambda_q2': nrm((ATT_QK,), 0.1),
        'lambda_k2': nrm((ATT_QK,), 0.1),
        'g_subln': gain(ATT_V),
        'g_mem': gain(D_MODEL),
        'w_mem_kv': nrm((D_MODEL, 2 * D_MEM), D_MODEL ** -0.5),
        'w_br_rwkv': nrm((D_RWKV, D_MODEL), D_RWKV ** -0.5),
        'w_br_attn': nrm((D_ATT, D_MODEL), D_ATT ** -0.5),
        'w_br_mem': nrm((D_MEM, D_MODEL), D_MEM ** -0.5),
        'w_out': nrm((D_MODEL, D_MODEL), D_MODEL ** -0.5),
        'g_post_mix': gain(D_MODEL),
        'g_pre_mlp': gain(D_MODEL),
        'w_up': nrm((D_MODEL, D_FF), D_MODEL ** -0.5),
        'w_down': nrm((D_FF, D_MODEL), D_FF ** -0.5),
        'g_post_mlp': gain(D_MODEL),
    }


def reference(x_prompt, x_sample, cache_k, cache_v, page_table, state_wkv, state_shift,
              cache_mem_k, cache_mem_v, mem_prompt, g_pre_mix, w_in, mu_shift, w0, w_lora_up,
              a0, a_lora_up, g_lora_up, k_k, k_a, r_k, ln_x_w, ln_x_b, lambda_q1, lambda_k1,
              lambda_q2, lambda_k2, g_subln, g_mem, w_mem_kv, w_br_rwkv, w_br_attn, w_br_mem,
              w_out, g_post_mix, g_pre_mlp, w_up, w_down, g_post_mlp):
    W = dict(g_pre_mix=g_pre_mix, w_in=w_in, mu_shift=mu_shift, w0=w0, w_lora_up=w_lora_up,
             a0=a0, a_lora_up=a_lora_up, g_lora_up=g_lora_up, k_k=k_k, k_a=k_a, r_k=r_k,
             ln_x_w=ln_x_w, ln_x_b=ln_x_b, lambda_q1=lambda_q1, lambda_k1=lambda_k1,
             lambda_q2=lambda_q2, lambda_k2=lambda_k2, g_subln=g_subln, w_br_rwkv=w_br_rwkv,
             w_br_attn=w_br_attn, w_br_mem=w_br_mem, w_out=w_out, g_post_mix=g_post_mix,
             g_pre_mlp=g_pre_mlp, w_up=w_up, w_down=w_down, g_post_mlp=g_post_mlp)
    slopes = alibi_slopes()
    B = x_prompt.shape[0]
    y_prompt, y_sample = x_prompt, x_sample
    for layer_idx in range(DEPTH):
        lam_init = 0.8 - 0.6 * math.exp(-0.3 * layer_idx)
        mem_k_prompt, mem_v_prompt = memory_kv(mem_prompt, g_mem, w_mem_kv)
        y_prompt, k_prompt, v_prompt, wkv_prompt, shift_prompt = layer(
            y_prompt, jnp.zeros((B, RW_COLS), x_prompt.dtype),
            jnp.zeros((B, RW_HEADS, RW_HEAD, RW_HEAD), jnp.float32),
            mem_k_prompt, mem_v_prompt,
            lambda q, k, v, lam: diff_attn_prompt(q, k, v, lam, slopes), lam_init, W)
        y_sample, k_sample, v_sample, wkv_sample, shift_sample = layer(
            y_sample, state_shift, state_wkv, cache_mem_k, cache_mem_v,
            lambda q, k, v, lam: diff_attn_sample(q, k, v, cache_k, cache_v, page_table, lam, slopes),
            lam_init, W)
    return (y_prompt, y_sample, k_prompt, v_prompt, k_sample, v_sample,
            wkv_prompt, wkv_sample, shift_prompt, shift_sample, mem_k_prompt, mem_v_prompt)
```

```python
import functools
import math

import jax
import jax.numpy as jnp
from jax import lax
from jax.experimental import pallas as pl
from jax.experimental.pallas import tpu as pltpu

F32 = jnp.float32
BF16 = jnp.bfloat16

D_MODEL = 4096
PAGE_SIZE = 128
D_RWKV = 2048
RW_HEAD = 64
RW_HEADS = 32
DECAY_LORA = 96
AAA_LORA = 96
GATE_LORA = 256
RW_COLS = 3 * D_RWKV + DECAY_LORA + AAA_LORA + GATE_LORA
RW_COLS_PAD = 6656
GN_EPS = 64e-5
D_ATT = 2048
ATT_HEADS = 8
ATT_QK = 128
ATT_V = 256
Q_BLOCK = 128
MEM_HEADS = 4
MEM_HEAD = 128
D_MEM = 512
D_FF = 4 * D_MODEL
NORM_EPS = 1e-6
NEG_INF = -1e30
LAM_INIT = 0.8 - 0.6 * math.exp(-0.3 * 0)

VMEM_LIMIT = 56 * 1024 * 1024


def _cparams(sem):
    return pltpu.CompilerParams(dimension_semantics=sem, vmem_limit_bytes=VMEM_LIMIT)


def _rmsnorm_kernel(x_ref, g_ref, o_ref):
    x = x_ref[...]
    ms = jnp.mean(x * x, axis=-1, keepdims=True)
    o_ref[...] = (x * lax.rsqrt(ms + NORM_EPS) * g_ref[...]).astype(o_ref.dtype)


def rmsnorm_bf16(x, g, tr=256):
    m, d = x.shape
    tr = min(tr, m)
    return pl.pallas_call(
        _rmsnorm_kernel,
        grid=(m // tr,),
        in_specs=[pl.BlockSpec((tr, d), lambda i: (i, 0)),
                  pl.BlockSpec((1, d), lambda i: (0, 0))],
        out_specs=pl.BlockSpec((tr, d), lambda i: (i, 0)),
        out_shape=jax.ShapeDtypeStruct((m, d), BF16),
        compiler_params=_cparams(("parallel",)),
        name="rmsnorm_bf16",
    )(x, g.reshape(1, d))


def _mm_kernel(a_ref, b_ref, o_ref, acc_ref, *, relu2):
    k = pl.program_id(2)

    @pl.when(k == 0)
    def _():
        acc_ref[...] = jnp.zeros_like(acc_ref)

    acc_ref[...] += jnp.dot(a_ref[...], b_ref[...].astype(BF16), preferred_element_type=F32)

    @pl.when(k == pl.num_programs(2) - 1)
    def _():
        r = acc_ref[...]
        if relu2:
            r = jnp.square(jnp.maximum(r, 0.0))
        o_ref[...] = r.astype(o_ref.dtype)


def matmul(a, b, *, n_cols=None, col0_blocks=0, tm=512, tn=512, tk=1024, out_dtype=F32, relu2=False):
    m, kd = a.shape
    n = b.shape[1] if n_cols is None else n_cols
    tm = min(tm, m)
    tk = min(tk, kd)
    assert m % tm == 0 and n % tn == 0 and kd % tk == 0
    return pl.pallas_call(
        functools.partial(_mm_kernel, relu2=relu2),
        grid=(m // tm, n // tn, kd // tk),
        in_specs=[pl.BlockSpec((tm, tk), lambda i, j, k: (i, k)),
                  pl.BlockSpec((tk, tn), lambda i, j, k: (k, j + col0_blocks))],
        out_specs=pl.BlockSpec((tm, tn), lambda i, j, k: (i, j)),
        out_shape=jax.ShapeDtypeStruct((m, n), out_dtype),
        scratch_shapes=[pltpu.VMEM((tm, tn), F32)],
        compiler_params=_cparams(("parallel", "parallel", "arbitrary")),
        name="matmul",
    )(a, b)


def _mm_norm_res_kernel(a_ref, b_ref, g_ref, x_ref, o_ref, acc_ref):
    k = pl.program_id(1)

    @pl.when(k == 0)
    def _():
        acc_ref[...] = jnp.zeros_like(acc_ref)

    acc_ref[...] += jnp.dot(a_ref[...], b_ref[...].astype(BF16), preferred_element_type=F32)

    @pl.when(k == pl.num_programs(1) - 1)
    def _():
        r = acc_ref[...]
        ms = jnp.mean(r * r, axis=-1, keepdims=True)
        o_ref[...] = x_ref[...] + r * lax.rsqrt(ms + NORM_EPS) * g_ref[...]


def matmul_norm_residual(a, b, g, x, *, tm=256, tk=512):
    m, kd = a.shape
    n = b.shape[1]
    tm = min(tm, m)
    return pl.pallas_call(
        _mm_norm_res_kernel,
        grid=(m // tm, kd // tk),
        in_specs=[pl.BlockSpec((tm, tk), lambda i, k: (i, k)),
                  pl.BlockSpec((tk, n), lambda i, k: (k, 0)),
                  pl.BlockSpec((1, n), lambda i, k: (0, 0)),
                  pl.BlockSpec((tm, n), lambda i, k: (i, 0))],
        out_specs=pl.BlockSpec((tm, n), lambda i, k: (i, 0)),
        out_shape=jax.ShapeDtypeStruct((m, n), F32),
        scratch_shapes=[pltpu.VMEM((tm, n), F32)],
        compiler_params=_cparams(("parallel", "arbitrary")),
        name="matmul_norm_residual",
    )(a, b, g.reshape(1, n), x)


def _merge_kernel(orw_ref, oatt_ref, omem_ref, wr_ref, wa_ref, wm_ref, grw_ref, gatt_ref, gmem_ref, o_ref):
    def branch(o_ref_, w_ref_, g_ref_):
        y = jnp.dot(o_ref_[...], w_ref_[...].astype(BF16), preferred_element_type=F32)
        return jax.nn.sigmoid(g_ref_[...]) * y

    m = branch(orw_ref, wr_ref, grw_ref) + branch(oatt_ref, wa_ref, gatt_ref) + branch(omem_ref, wm_ref, gmem_ref)
    o_ref[...] = m.astype(o_ref.dtype)


def gated_merge(o_rw, o_att, o_mem, w_r, w_a, w_m, p_b, gate_col0, *, tm=512, tn=512):
    m = o_rw.shape[0]
    tm = min(tm, m)
    gb = gate_col0 // tn
    nb = D_MODEL // tn
    assert gate_col0 % tn == 0

    def gate_spec(b):
        return pl.BlockSpec((tm, tn), lambda i, j: (i, gb + b * nb + j))

    return pl.pallas_call(
        _merge_kernel,
        grid=(m // tm, D_MODEL // tn),
        in_specs=[pl.BlockSpec((tm, D_RWKV), lambda i, j: (i, 0)),
                  pl.BlockSpec((tm, D_ATT), lambda i, j: (i, 0)),
                  pl.BlockSpec((tm, D_MEM), lambda i, j: (i, 0)),
                  pl.BlockSpec((D_RWKV, tn), lambda i, j: (0, j)),
                  pl.BlockSpec((D_ATT, tn), lambda i, j: (0, j)),
                  pl.BlockSpec((D_MEM, tn), lambda i, j: (0, j)),
                  gate_spec(0), gate_spec(1), gate_spec(2)],
        out_specs=pl.BlockSpec((tm, tn), lambda i, j: (i, j)),
        out_shape=jax.ShapeDtypeStruct((m, D_MODEL), BF16),
        compiler_params=_cparams(("parallel", "parallel")),
        name="gated_merge",
    )(o_rw, o_att, o_mem, w_r, w_a, w_m, p_b, p_b, p_b)


def _rwkv7_branch(p, shift_prev, s0, mu_shift, w0, w_lora_up, a0, a_lora_up, g_lora_up,
                  k_k, k_a, r_k, ln_x_w, ln_x_b):
    B, T, _ = p.shape
    prev = jnp.concatenate([shift_prev[:, None], p[:, :-1]], axis=1)
    xs = p + (prev - p) * mu_shift
    r, k, v, wd, ad, gd = jnp.split(
        xs, [D_RWKV, 2 * D_RWKV, 3 * D_RWKV, 3 * D_RWKV + DECAY_LORA,
             3 * D_RWKV + DECAY_LORA + AAA_LORA], axis=-1)
    w = -jax.nn.softplus(-(w0 + jnp.tanh(wd) @ w_lora_up)) - 0.5
    decay = jnp.exp(-jnp.exp(w))
    a = jax.nn.sigmoid(a0 + ad @ a_lora_up)
    g = jax.nn.sigmoid(gd) @ g_lora_up
    hd = lambda t: t.reshape(B, T, RW_HEADS, RW_HEAD)
    kk = hd(k * k_k)
    kk = kk / jnp.maximum(jnp.sqrt(jnp.sum(kk * kk, axis=-1, keepdims=True)), 1e-12)
    k_mod = k * (1.0 + (a - 1.0) * k_a)
    rh, kh, vh, ah, wh = hd(r), hd(k_mod), hd(v), hd(a), hd(decay)
    tm = lambda t: jnp.moveaxis(t, 1, 0)

    def step(s, inp):
        w_t, r_t, k_t, v_t, kk_t, b_t = inp
        s_kk = jnp.einsum('bhvk,bhk->bhv', s, kk_t)
        s = (s * w_t[:, :, None, :] - s_kk[..., None] * b_t[:, :, None, :]
             + v_t[..., None] * k_t[:, :, None, :])
        return s, jnp.einsum('bhvk,bhk->bhv', s, r_t)

    s_fin, y = lax.scan(step, s0, (tm(wh), tm(rh), tm(kh), tm(vh), tm(kk), tm(kk * ah)))
    y = tm(y)
    mu = jnp.mean(y, axis=-1, keepdims=True)
    var = jnp.mean(jnp.square(y - mu), axis=-1, keepdims=True)
    yn = ((y - mu) * lax.rsqrt(var + GN_EPS)).reshape(B, T, D_RWKV)
    yn = yn * ln_x_w + ln_x_b
    bonus = (jnp.sum(rh * kh * r_k, axis=-1, keepdims=True) * vh).reshape(B, T, D_RWKV)
    out = (yn + bonus) * g
    return out, s_fin, p[:, -1]


def _diff_core(q, k, v, q_pos, k_pos, lam, slopes):
    s = jnp.einsum('bqhcd,blhcd->bhcql', q, k) * (ATT_QK ** -0.5)
    dist = (q_pos[:, None] - k_pos[None, :]).astype(F32)
    bias = jnp.where(dist >= 0, -slopes[:, None, None] * dist, NEG_INF)
    p = jax.nn.softmax(s + bias[None, :, None], axis=-1)
    attn = p[:, :, 0] - lam * p[:, :, 1]
    return jnp.einsum('bhql,blhe->bqhe', attn, v)


def _diff_attn_prompt(q, k, v, lam, slopes):
    B, T = q.shape[:2]
    nb = T // Q_BLOCK
    qb = jnp.swapaxes(q.reshape(B, nb, Q_BLOCK, ATT_HEADS, 2, ATT_QK), 0, 1)
    k_pos = jnp.arange(T)

    def block(args):
        q_i, i = args
        return _diff_core(q_i, k, v, i * Q_BLOCK + jnp.arange(Q_BLOCK), k_pos, lam, slopes)

    o = lax.map(block, (qb, jnp.arange(nb)))
    return jnp.swapaxes(o, 0, 1).reshape(B, T, ATT_HEADS, ATT_V)


def _diff_attn_sample(q, k_new, v_new, cache_k, cache_v, page_table, lam, slopes):
    DS = q.shape[1]
    past = page_table.shape[1] * PAGE_SIZE
    k_pos = jnp.arange(past + DS)
    q_pos = past + jnp.arange(DS)

    def one(args):
        q_b, kn_b, vn_b, pt_b = args
        kp = cache_k[pt_b].reshape(past, ATT_HEADS, 2, ATT_QK)
        vp = cache_v[pt_b].reshape(past, ATT_HEADS, ATT_V)
        kk = jnp.concatenate([kp, kn_b], axis=0)
        vv = jnp.concatenate([vp, vn_b], axis=0)
        return _diff_core(q_b[None], kk[None], vv[None], q_pos, k_pos, lam, slopes)[0]

    return lax.map(one, (q, k_new, v_new, page_table))


def _memory_attn(qm, mk, mv):
    B, T = qm.shape[:2]
    s = jnp.einsum('bthe,bmhe->bhtm', qm, mk) * (MEM_HEAD ** -0.5)
    p = jax.nn.softmax(s, axis=-1)
    return jnp.einsum('bhtm,bmhe->bthe', p, mv).reshape(B, T, D_MEM)


def _rmsnorm(x, g):
    return x * lax.rsqrt(jnp.mean(x * x, axis=-1, keepdims=True) + NORM_EPS) * g


def _layer(x, shift_prev, s0, mem_k, mem_v, attend, W):
    B, T, _ = x.shape
    M = B * T
    x2 = x.reshape(M, D_MODEL)
    h = rmsnorm_bf16(x2, W['g_pre_mix'])
    p_a = matmul(h, W['w_in'], n_cols=RW_COLS_PAD)
    p_b = matmul(h, W['w_in_b'])
    p_rw = p_a[:, :RW_COLS].reshape(B, T, RW_COLS)
    c2 = D_ATT
    c3 = c2 + D_ATT
    c4 = c3 + D_ATT
    c5 = c4 + D_MEM
    q = p_b[:, :c2].reshape(B, T, ATT_HEADS, 2, ATT_QK)
    k = p_b[:, c2:c3].reshape(B, T, ATT_HEADS, 2, ATT_QK)
    v = p_b[:, c3:c4].reshape(B, T, ATT_HEADS, ATT_V)
    qm = p_b[:, c4:c5].reshape(B, T, MEM_HEADS, MEM_HEAD)

    o_rw, s_fin, shift_last = _rwkv7_branch(
        p_rw, shift_prev, s0, W['mu_shift'], W['w0'], W['w_lora_up'], W['a0'], W['a_lora_up'],
        W['g_lora_up'], W['k_k'], W['k_a'], W['r_k'], W['ln_x_w'], W['ln_x_b'])
    lam = (jnp.exp(jnp.sum(W['lambda_q1'] * W['lambda_k1']))
           - jnp.exp(jnp.sum(W['lambda_q2'] * W['lambda_k2'])) + LAM_INIT)
    o = attend(q, k, v, lam)
    o_att = (_rmsnorm(o, W['g_subln']) * (1.0 - LAM_INIT)).reshape(M, D_ATT)
    o_mem = _memory_attn(qm, mem_k, mem_v).reshape(M, D_MEM)

    m = gated_merge(o_rw.reshape(M, D_RWKV).astype(BF16), o_att.astype(BF16), o_mem.astype(BF16),
                    W['w_br_rwkv'], W['w_br_attn'], W['w_br_mem'], p_b, c5)
    x1 = matmul_norm_residual(m, W['w_out'], W['g_post_mix'], x2)
    h2 = rmsnorm_bf16(x1, W['g_pre_mlp'])
    u = matmul(h2, W['w_up'], out_dtype=BF16, relu2=True)
    y = matmul_norm_residual(u, W['w_down'], W['g_post_mlp'], x1)
    return (y.reshape(B, T, D_MODEL), k.reshape(B, T, ATT_HEADS, 2 * ATT_QK), v, s_fin, shift_last)


def kernel(x_prompt, x_sample, cache_k, cache_v, page_table, state_wkv, state_shift, cache_mem_k, cache_mem_v, mem_prompt, g_pre_mix, w_in, mu_shift, w0, w_lora_up, a0, a_lora_up, g_lora_up, k_k, k_a, r_k, ln_x_w, ln_x_b, lambda_q1, lambda_k1, lambda_q2, lambda_k2, g_subln, g_mem, w_mem_kv, w_br_rwkv, w_br_attn, w_br_mem, w_out, g_post_mix, g_pre_mlp, w_up, w_down, g_post_mlp):
    W = dict(g_pre_mix=g_pre_mix, w_in=w_in, w_in_b=w_in[:, RW_COLS:].astype(BF16),
             mu_shift=mu_shift, w0=w0, w_lora_up=w_lora_up,
             a0=a0, a_lora_up=a_lora_up, g_lora_up=g_lora_up, k_k=k_k, k_a=k_a, r_k=r_k,
             ln_x_w=ln_x_w, ln_x_b=ln_x_b, lambda_q1=lambda_q1, lambda_k1=lambda_k1,
             lambda_q2=lambda_q2, lambda_k2=lambda_k2, g_subln=g_subln, w_br_rwkv=w_br_rwkv,
             w_br_attn=w_br_attn, w_br_mem=w_br_mem, w_out=w_out, g_post_mix=g_post_mix,
             g_pre_mlp=g_pre_mlp, w_up=w_up, w_down=w_down, g_post_mlp=g_post_mlp)
    slopes = jnp.asarray([2.0 ** (-8.0 * (h + 1) / ATT_HEADS) for h in range(ATT_HEADS)], dtype=F32)
    B = x_prompt.shape[0]
    MT = mem_prompt.shape[1]

    hm = rmsnorm_bf16(mem_prompt.reshape(B * MT, D_MODEL), g_mem)
    kv = matmul(hm, w_mem_kv)
    mem_k_prompt = kv[:, :D_MEM].reshape(B, MT, MEM_HEADS, MEM_HEAD)
    mem_v_prompt = kv[:, D_MEM:].reshape(B, MT, MEM_HEADS, MEM_HEAD)

    y_prompt, k_prompt, v_prompt, wkv_prompt, shift_prompt = _layer(
        x_prompt, jnp.zeros((B, RW_COLS), F32), jnp.zeros((B, RW_HEADS, RW_HEAD, RW_HEAD), F32),
        mem_k_prompt, mem_v_prompt,
        lambda q, k, v, lam: _diff_attn_prompt(q, k, v, lam, slopes), W)
    y_sample, k_sample, v_sample, wkv_sample, shift_sample = _layer(
        x_sample, state_shift, state_wkv, cache_mem_k, cache_mem_v,
        lambda q, k, v, lam: _diff_attn_sample(q, k, v, cache_k, cache_v, page_table, lam, slopes), W)
    return (y_prompt, y_sample, k_prompt, v_prompt, k_sample, v_sample,
            wkv_prompt, wkv_sample, shift_prompt, shift_sample, mem_k_prompt, mem_v_prompt)
```

```python
import functools
import math

import jax
import jax.numpy as jnp
from jax import lax
from jax.experimental import pallas as pl
from jax.experimental.pallas import tpu as pltpu

F32 = jnp.float32
BF16 = jnp.bfloat16

D_MODEL = 4096
PAGE_SIZE = 128
D_RWKV = 2048
RW_HEAD = 64
RW_HEADS = 32
DECAY_LORA = 96
AAA_LORA = 96
GATE_LORA = 256
RW_COLS = 3 * D_RWKV + DECAY_LORA + AAA_LORA + GATE_LORA
RW_COLS_PAD = 6656
GN_EPS = 64e-5
D_ATT = 2048
ATT_HEADS = 8
ATT_QK = 128
ATT_V = 256
Q_BLOCK = 128
MEM_HEADS = 4
MEM_HEAD = 128
D_MEM = 512
D_FF = 4 * D_MODEL
NORM_EPS = 1e-6
NEG_INF = -1e30
LAM_INIT = 0.8 - 0.6 * math.exp(-0.3 * 0)

VMEM_LIMIT = 56 * 1024 * 1024


def _cparams(sem):
    return pltpu.CompilerParams(dimension_semantics=sem, vmem_limit_bytes=VMEM_LIMIT)


def _rmsnorm_kernel(x_ref, g_ref, o_ref):
    x = x_ref[...]
    ms = jnp.mean(x * x, axis=-1, keepdims=True)
    o_ref[...] = (x * lax.rsqrt(ms + NORM_EPS) * g_ref[...]).astype(o_ref.dtype)


def rmsnorm_bf16(x, g, tr=256):
    m, d = x.shape
    tr = min(tr, m)
    return pl.pallas_call(
        _rmsnorm_kernel,
        grid=(m // tr,),
        in_specs=[pl.BlockSpec((tr, d), lambda i: (i, 0)),
                  pl.BlockSpec((1, d), lambda i: (0, 0))],
        out_specs=pl.BlockSpec((tr, d), lambda i: (i, 0)),
        out_shape=jax.ShapeDtypeStruct((m, d), BF16),
        compiler_params=_cparams(("parallel",)),
        name="rmsnorm_bf16",
    )(x, g.reshape(1, d))


def _mm_kernel(a_ref, b_ref, o_ref, *scratch, relu2, nk):
    def finish(r):
        if relu2:
            r = jnp.square(jnp.maximum(r, 0.0))
        o_ref[...] = r.astype(o_ref.dtype)

    part = jnp.dot(a_ref[...], b_ref[...].astype(BF16), preferred_element_type=F32)
    if nk == 1:
        finish(part)
        return
    acc_ref, = scratch
    k = pl.program_id(2)

    @pl.when(k == 0)
    def _():
        acc_ref[...] = part

    @pl.when(k > 0)
    def _():
        acc_ref[...] += part

    @pl.when(k == nk - 1)
    def _():
        finish(acc_ref[...])


def matmul(a, b, *, n_cols=None, tm=1024, tn=512, tk=4096, out_dtype=F32, relu2=False):
    m, kd = a.shape
    n = b.shape[1] if n_cols is None else n_cols
    tm = min(tm, m)
    tk = min(tk, kd)
    assert m % tm == 0 and n % tn == 0 and kd % tk == 0
    nk = kd // tk
    return pl.pallas_call(
        functools.partial(_mm_kernel, relu2=relu2, nk=nk),
        grid=(m // tm, n // tn, nk),
        in_specs=[pl.BlockSpec((tm, tk), lambda i, j, k: (i, k)),
                  pl.BlockSpec((tk, tn), lambda i, j, k: (k, j))],
        out_specs=pl.BlockSpec((tm, tn), lambda i, j, k: (i, j)),
        out_shape=jax.ShapeDtypeStruct((m, n), out_dtype),
        scratch_shapes=[pltpu.VMEM((tm, tn), F32)] if nk > 1 else [],
        compiler_params=_cparams(("parallel", "parallel", "arbitrary")),
        name="matmul",
    )(a, b)


def _norm_res_kernel(z_ref, g_ref, x_ref, *rest, with_next):
    z = z_ref[...]
    ms = jnp.mean(z * z, axis=-1, keepdims=True)
    y = x_ref[...] + z * lax.rsqrt(ms + NORM_EPS) * g_ref[...]
    if with_next:
        g2_ref, o_ref, h_ref = rest
        ms2 = jnp.mean(y * y, axis=-1, keepdims=True)
        h_ref[...] = (y * lax.rsqrt(ms2 + NORM_EPS) * g2_ref[...]).astype(h_ref.dtype)
    else:
        o_ref, = rest
    o_ref[...] = y


def norm_residual(z, g, x, g_next=None, tr=256):
    m, d = z.shape
    tr = min(tr, m)
    row = pl.BlockSpec((tr, d), lambda i: (i, 0))
    vec = pl.BlockSpec((1, d), lambda i: (0, 0))
    with_next = g_next is not None
    args = (z, g.reshape(1, d), x) + ((g_next.reshape(1, d),) if with_next else ())
    return pl.pallas_call(
        functools.partial(_norm_res_kernel, with_next=with_next),
        grid=(m // tr,),
        in_specs=[row, vec, row] + ([vec] if with_next else []),
        out_specs=[row, row] if with_next else row,
        out_shape=([jax.ShapeDtypeStruct((m, d), F32), jax.ShapeDtypeStruct((m, d), BF16)] if with_next
                   else jax.ShapeDtypeStruct((m, d), F32)),
        compiler_params=_cparams(("parallel",)),
        name="norm_residual",
    )(*args)


def _merge_kernel(orw_ref, oatt_ref, omem_ref, wr_ref, wa_ref, wm_ref, grw_ref, gatt_ref, gmem_ref, o_ref):
    def branch(o_ref_, w_ref_, g_ref_):
        y = jnp.dot(o_ref_[...], w_ref_[...].astype(BF16), preferred_element_type=F32)
        return jax.nn.sigmoid(g_ref_[...]) * y

    m = branch(orw_ref, wr_ref, grw_ref) + branch(oatt_ref, wa_ref, gatt_ref) + branch(omem_ref, wm_ref, gmem_ref)
    o_ref[...] = m.astype(o_ref.dtype)


def gated_merge(o_rw, o_att, o_mem, w_r, w_a, w_m, p_b, gate_col0, *, tm=512, tn=512):
    m = o_rw.shape[0]
    tm = min(tm, m)
    gb = gate_col0 // tn
    nb = D_MODEL // tn
    assert gate_col0 % tn == 0

    def gate_spec(b):
        return pl.BlockSpec((tm, tn), lambda i, j: (i, gb + b * nb + j))

    return pl.pallas_call(
        _merge_kernel,
        grid=(m // tm, D_MODEL // tn),
        in_specs=[pl.BlockSpec((tm, D_RWKV), lambda i, j: (i, 0)),
                  pl.BlockSpec((tm, D_ATT), lambda i, j: (i, 0)),
                  pl.BlockSpec((tm, D_MEM), lambda i, j: (i, 0)),
                  pl.BlockSpec((D_RWKV, tn), lambda i, j: (0, j)),
                  pl.BlockSpec((D_ATT, tn), lambda i, j: (0, j)),
                  pl.BlockSpec((D_MEM, tn), lambda i, j: (0, j)),
                  gate_spec(0), gate_spec(1), gate_spec(2)],
        out_specs=pl.BlockSpec((tm, tn), lambda i, j: (i, j)),
        out_shape=jax.ShapeDtypeStruct((m, D_MODEL), BF16),
        compiler_params=_cparams(("parallel", "parallel")),
        name="gated_merge",
    )(o_rw, o_att, o_mem, w_r, w_a, w_m, p_b, p_b, p_b)


RW_QUAD = 4 * RW_HEAD
RW_NQ = D_RWKV // RW_QUAD
RW_ROWS = RW_NQ * RW_HEAD
Y_TILE = 64


def _scan_kernel(w_ref, r_ref, k_ref, v_ref, kk_ref, b_ref, s0_ref, eye_ref, ones_ref, ones2_ref,
                 y_ref, s_ref, *, tc):
    c = pl.program_id(1)

    @pl.when(c == 0)
    def _():
        s_ref[0] = s0_ref[0]

    y_ref[0, 0] = jnp.zeros((RW_ROWS, RW_QUAD), F32)
    lane_t = lax.broadcasted_iota(jnp.int32, (1, RW_QUAD), 1) % RW_HEAD

    def stack(ref, t):
        row = ref[0, pl.ds(t, 1), :]
        return jnp.concatenate(
            [jnp.broadcast_to(row[:, q * RW_QUAD:(q + 1) * RW_QUAD], (RW_HEAD, RW_QUAD)) for q in range(RW_NQ)],
            axis=0)

    def seg_sum(x):
        hi = x.astype(BF16)
        lo = (x - hi.astype(F32)).astype(BF16)
        return jnp.dot(jnp.concatenate([hi, lo], axis=1), ones2_ref[...], preferred_element_type=F32)

    def body(t, carry):
        s = s_ref[0]
        s_kk = seg_sum(s * stack(kk_ref, t))
        v_col = seg_sum(eye_ref[...] * stack(v_ref, t))
        s = s * stack(w_ref, t) - s_kk * stack(b_ref, t) + v_col * stack(k_ref, t)
        s_ref[0] = s
        y_b = jnp.dot((s * stack(r_ref, t)).astype(BF16), ones_ref[...], preferred_element_type=F32)
        y_ref[0, 0] = jnp.where(lane_t == t, y_b, y_ref[0, 0])
        return carry

    lax.fori_loop(0, tc, body, 0, unroll=2)


def rwkv_scan(w, r, k, v, kk, b, s0):
    B, T, _ = w.shape
    tc = min(Y_TILE, T)
    nc = T // tc
    assert T % tc == 0
    s0q = s0.reshape(B, RW_NQ, 4, RW_HEAD, RW_HEAD).transpose(0, 1, 3, 2, 4).reshape(B, RW_ROWS, RW_QUAD)
    rr = lax.broadcasted_iota(jnp.int32, (RW_ROWS, RW_QUAD), 0) % RW_HEAD
    cc = lax.broadcasted_iota(jnp.int32, (RW_ROWS, RW_QUAD), 1)
    eye = (rr == cc % RW_HEAD).astype(F32)
    r2 = lax.broadcasted_iota(jnp.int32, (RW_QUAD, RW_QUAD), 0) // RW_HEAD
    c2 = lax.broadcasted_iota(jnp.int32, (RW_QUAD, RW_QUAD), 1) // RW_HEAD
    ones = (r2 == c2).astype(BF16)
    ones2 = jnp.concatenate([ones, ones], axis=0)
    xspec = pl.BlockSpec((1, tc, D_RWKV), lambda bi, ci: (bi, ci, 0))
    const = lambda shape: pl.BlockSpec(shape, lambda bi, ci: (0,) * len(shape))
    y_t, s_fin = pl.pallas_call(
        functools.partial(_scan_kernel, tc=tc),
        grid=(B, nc),
        in_specs=[xspec] * 6 + [pl.BlockSpec((1, RW_ROWS, RW_QUAD), lambda bi, ci: (bi, 0, 0)),
                                const((RW_ROWS, RW_QUAD)), const((RW_QUAD, RW_QUAD)),
                                const((2 * RW_QUAD, RW_QUAD))],
        out_specs=[pl.BlockSpec((1, 1, RW_ROWS, RW_QUAD), lambda bi, ci: (bi, ci, 0, 0)),
                   pl.BlockSpec((1, RW_ROWS, RW_QUAD), lambda bi, ci: (bi, 0, 0))],
        out_shape=[jax.ShapeDtypeStruct((B, nc, RW_ROWS, RW_QUAD), F32),
                   jax.ShapeDtypeStruct((B, RW_ROWS, RW_QUAD), F32)],
        compiler_params=_cparams(("parallel", "arbitrary")),
        name="rwkv_scan",
    )(w, r, k, v, kk, b, s0q, eye, ones, ones2)
    y = y_t.reshape(B, nc, RW_NQ, RW_HEAD, 4, RW_HEAD)[..., :tc].transpose(0, 1, 5, 2, 4, 3).reshape(B, T, D_RWKV)
    s_fin = s_fin.reshape(B, RW_NQ, RW_HEAD, 4, RW_HEAD).transpose(0, 1, 3, 2, 4).reshape(B, RW_HEADS, RW_HEAD, RW_HEAD)
    return y, s_fin


ATT_SCALE = ATT_QK ** -0.5
DOT_NT = (((1,), (1,)), ((), ()))


def _lam_from(lamp_ref):
    lp = lamp_ref[...]
    return (jnp.exp(jnp.sum(lp[0:1] * lp[1:2])) - jnp.exp(jnp.sum(lp[2:3] * lp[3:4])) + LAM_INIT)


def _online_softmax_step(s, v_bf, m, l, a):
    m_new = jnp.maximum(m, jnp.max(s, axis=-1, keepdims=True))
    alpha = jnp.exp(m - m_new)
    p = jnp.exp(s - m_new)
    l = alpha * l + jnp.sum(p, axis=-1, keepdims=True)
    a = alpha * a + jnp.dot(p.astype(BF16), v_bf, preferred_element_type=F32)
    return m_new, l, a


def _subln(o, g):
    ms = jnp.mean(o * o, axis=-1, keepdims=True)
    return o * lax.rsqrt(ms + NORM_EPS) * g * (1.0 - LAM_INIT)


def _diff_prompt_kernel(slopes_ref, q_ref, k_ref, v_ref, lamp_ref, g_ref, o_ref, *, tq, tk):
    h = pl.program_id(1)
    qi = pl.program_id(2)
    slope = slopes_ref[h]
    lam = _lam_from(lamp_ref)
    q = q_ref[...]
    q1 = q[:, :ATT_QK].astype(BF16)
    q2 = q[:, ATT_QK:].astype(BF16)
    qpos = qi * tq + lax.broadcasted_iota(jnp.int32, (tq, 1), 0)

    def kv_step(kb, carry):
        m1, l1, a1, m2, l2, a2 = carry
        k0 = pl.multiple_of(kb * tk, tk)
        k = k_ref[pl.ds(k0, tk), :]
        v = v_ref[pl.ds(k0, tk), :].astype(BF16)
        kpos = k0 + lax.broadcasted_iota(jnp.int32, (1, tk), 1)
        dist = (qpos - kpos).astype(F32)
        bias = jnp.where(dist >= 0, -slope * dist, NEG_INF)
        s1 = lax.dot_general(q1, k[:, :ATT_QK].astype(BF16), DOT_NT, preferred_element_type=F32) * ATT_SCALE + bias
        s2 = lax.dot_general(q2, k[:, ATT_QK:].astype(BF16), DOT_NT, preferred_element_type=F32) * ATT_SCALE + bias
        m1, l1, a1 = _online_softmax_step(s1, v, m1, l1, a1)
        m2, l2, a2 = _online_softmax_step(s2, v, m2, l2, a2)
        return m1, l1, a1, m2, l2, a2

    col = lambda val: jnp.full((tq, 1), val, F32)
    acc0 = jnp.zeros((tq, ATT_V), F32)
    n_kb = (qi * tq + tq + tk - 1) // tk
    m1, l1, a1, m2, l2, a2 = lax.fori_loop(
        0, n_kb, kv_step, (col(NEG_INF), col(0.0), acc0, col(NEG_INF), col(0.0), acc0))
    o = a1 / l1 - lam * (a2 / l2)
    o_ref[...] = _subln(o, g_ref[...]).astype(o_ref.dtype)


def diff_attn_prompt(p_b, B, T, slopes, lamp, g_subln, *, tq=256, tk=256):
    nq = T // tq
    hb = D_ATT // ATT_V
    return pl.pallas_call(
        functools.partial(_diff_prompt_kernel, tq=tq, tk=tk),
        grid=(B, ATT_HEADS, nq),
        in_specs=[pl.BlockSpec(memory_space=pltpu.SMEM),
                  pl.BlockSpec((tq, ATT_V), lambda b, h, i: (b * nq + i, h)),
                  pl.BlockSpec((T, ATT_V), lambda b, h, i: (b, hb + h)),
                  pl.BlockSpec((T, ATT_V), lambda b, h, i: (b, 2 * hb + h)),
                  pl.BlockSpec((4, ATT_QK), lambda b, h, i: (0, 0)),
                  pl.BlockSpec((1, ATT_V), lambda b, h, i: (0, 0))],
        out_specs=pl.BlockSpec((tq, ATT_V), lambda b, h, i: (b * nq + i, h)),
        out_shape=jax.ShapeDtypeStruct((B * T, D_ATT), BF16),
        compiler_params=_cparams(("parallel", "parallel", "arbitrary")),
        name="diff_attn_prompt",
    )(slopes, p_b, p_b, p_b, lamp, g_subln.reshape(1, ATT_V))


PAGES_PER_STEP = 4
QROWS = 16


def _diff_sample_kernel(pt_ref, slopes_ref, q_ref, kn_ref, vn_ref, lamp_ref, g_ref, *rest, ds, past):
    k_refs = rest[:PAGES_PER_STEP]
    v_refs = rest[PAGES_PER_STEP:2 * PAGES_PER_STEP]
    o_ref = rest[2 * PAGES_PER_STEP]
    qbd_ref, m_ref, l_ref, a_ref = rest[2 * PAGES_PER_STEP + 1:]
    j = pl.program_id(1)
    row = lax.broadcasted_iota(jnp.int32, (QROWS, 1), 0)
    qpos = past + row % ds

    @pl.when(j == 0)
    def _():
        lane = lax.broadcasted_iota(jnp.int32, (ds, ATT_V), 1)
        for h in range(ATT_HEADS):
            qh = q_ref[0, :, h * ATT_V:(h + 1) * ATT_V]
            rows = jnp.concatenate([jnp.where(lane < ATT_QK, qh, 0.0), jnp.where(lane >= ATT_QK, qh, 0.0),
                                    jnp.zeros((QROWS - 2 * ds, ATT_V), F32)], axis=0)
            qbd_ref[h] = rows.astype(BF16)
        m_ref[...] = jnp.full(m_ref.shape, NEG_INF, F32)
        l_ref[...] = jnp.zeros(l_ref.shape, F32)
        a_ref[...] = jnp.zeros(a_ref.shape, F32)

    def attend(k_list, v_list, kpos, valid):
        for h in range(ATT_HEADS):
            k = jnp.concatenate([r[0, :, h, :] for r in k_list], axis=0).astype(BF16)
            v = jnp.concatenate([r[0, :, h, :] for r in v_list], axis=0).astype(BF16)
            dist = (qpos - kpos).astype(F32)
            bias = -slopes_ref[h] * dist
            if valid is not None:
                bias = jnp.where(valid, bias, NEG_INF)
            s = lax.dot_general(qbd_ref[h], k, DOT_NT, preferred_element_type=F32) * ATT_SCALE + bias
            m, l, a = _online_softmax_step(s, v, m_ref[h], l_ref[h], a_ref[h])
            m_ref[h] = m
            l_ref[h] = l
            a_ref[h] = a

    n_keys = PAGES_PER_STEP * PAGE_SIZE
    kpos = j * n_keys + lax.broadcasted_iota(jnp.int32, (1, n_keys), 1)
    attend(k_refs, v_refs, kpos, None)

    @pl.when(j == pl.num_programs(1) - 1)
    def _():
        slot = lax.broadcasted_iota(jnp.int32, (1, PAGE_SIZE), 1)
        attend([kn_ref], [vn_ref], past + slot, (slot < ds) & (past + slot <= qpos))
        lam = _lam_from(lamp_ref)
        for h in range(ATT_HEADS):
            o = a_ref[h] / l_ref[h]
            o = o[0:ds] - lam * o[ds:2 * ds]
            o_ref[0, :, h * ATT_V:(h + 1) * ATT_V] = _subln(o, g_ref[...]).astype(o_ref.dtype)


def diff_attn_sample(q, k_new, v_new, cache_k, cache_v, page_table, slopes, lamp, g_subln):
    B, ds, _ = q.shape
    n_pages = page_table.shape[1]
    past = n_pages * PAGE_SIZE
    assert n_pages % PAGES_PER_STEP == 0 and 2 * ds <= QROWS
    pad = lambda t: jnp.pad(t, ((0, 0), (0, PAGE_SIZE - ds), (0, 0), (0, 0)))
    page_blk = (1, PAGE_SIZE, ATT_HEADS, ATT_V)

    def page_spec(i):
        return pl.BlockSpec(page_blk, lambda b, j, pt: (pt[b, j * PAGES_PER_STEP + i], 0, 0, 0))

    fixed = lambda shape: pl.BlockSpec(shape, lambda b, j, pt: (0,) * len(shape))
    grid_spec = pltpu.PrefetchScalarGridSpec(
        num_scalar_prefetch=1,
        grid=(B, n_pages // PAGES_PER_STEP),
        in_specs=[pl.BlockSpec(memory_space=pltpu.SMEM),
                  pl.BlockSpec((1, ds, D_ATT), lambda b, j, pt: (b, 0, 0)),
                  pl.BlockSpec(page_blk, lambda b, j, pt: (b, 0, 0, 0)),
                  pl.BlockSpec(page_blk, lambda b, j, pt: (b, 0, 0, 0)),
                  fixed((4, ATT_QK)), fixed((1, ATT_V))]
                 + [page_spec(i) for i in range(PAGES_PER_STEP)] * 2,
        out_specs=pl.BlockSpec((1, ds, D_ATT), lambda b, j, pt: (b, 0, 0)),
        scratch_shapes=[pltpu.VMEM((ATT_HEADS, QROWS, ATT_V), BF16),
                        pltpu.VMEM((ATT_HEADS, QROWS, 1), F32),
                        pltpu.VMEM((ATT_HEADS, QROWS, 1), F32),
                        pltpu.VMEM((ATT_HEADS, QROWS, ATT_V), F32)])
    return pl.pallas_call(
        functools.partial(_diff_sample_kernel, ds=ds, past=past),
        grid_spec=grid_spec,
        out_shape=jax.ShapeDtypeStruct((B, ds, D_ATT), BF16),
        compiler_params=_cparams(("parallel", "arbitrary")),
        name="diff_attn_sample",
    )(page_table, slopes, q, pad(k_new), pad(v_new), lamp, g_subln.reshape(1, ATT_V),
      *([cache_k] * PAGES_PER_STEP), *([cache_v] * PAGES_PER_STEP))


def _mem_attn_kernel(q_ref, k_ref, v_ref, o_ref, *, paged_heads, rows):
    for h in range(MEM_HEADS):
        sl = slice(h * MEM_HEAD, (h + 1) * MEM_HEAD)
        q = q_ref[:, sl] if q_ref.ndim == 2 else q_ref[0, :, sl]
        if q.shape[0] < rows:
            q = jnp.concatenate([q, jnp.zeros((rows - q.shape[0], MEM_HEAD), F32)], axis=0)
        k = k_ref[0, :, h, :] if paged_heads else k_ref[:, sl]
        v = v_ref[0, :, h, :] if paged_heads else v_ref[:, sl]
        s = lax.dot_general(q.astype(BF16), k.astype(BF16), DOT_NT, preferred_element_type=F32) * (MEM_HEAD ** -0.5)
        p = jnp.exp(s - jnp.max(s, axis=-1, keepdims=True))
        o = jnp.dot(p.astype(BF16), v.astype(BF16), preferred_element_type=F32) / jnp.sum(p, axis=-1, keepdims=True)
        if q_ref.ndim == 2:
            o_ref[:, sl] = o.astype(o_ref.dtype)
        else:
            o_ref[0, :, sl] = o[:q_ref.shape[1]].astype(o_ref.dtype)


def mem_attn_prompt(p_b, kv, B, T, MT, *, tq=512):
    tq = min(tq, T)
    nq = T // tq
    qcol = 3 * D_ATT // D_MEM
    return pl.pallas_call(
        functools.partial(_mem_attn_kernel, paged_heads=False, rows=tq),
        grid=(B, nq),
        in_specs=[pl.BlockSpec((tq, D_MEM), lambda b, i: (b * nq + i, qcol)),
                  pl.BlockSpec((MT, D_MEM), lambda b, i: (b, 0)),
                  pl.BlockSpec((MT, D_MEM), lambda b, i: (b, 1))],
        out_specs=pl.BlockSpec((tq, D_MEM), lambda b, i: (b * nq + i, 0)),
        out_shape=jax.ShapeDtypeStruct((B * T, D_MEM), BF16),
        compiler_params=_cparams(("parallel", "parallel")),
        name="mem_attn_prompt",
    )(p_b, kv, kv)


def mem_attn_sample(qm, mem_k, mem_v):
    B, ds, _ = qm.shape
    MT = mem_k.shape[1]
    kv_spec = pl.BlockSpec((1, MT, MEM_HEADS, MEM_HEAD), lambda b: (b, 0, 0, 0))
    return pl.pallas_call(
        functools.partial(_mem_attn_kernel, paged_heads=True, rows=QROWS),
        grid=(B,),
        in_specs=[pl.BlockSpec((1, ds, D_MEM), lambda b: (b, 0, 0)), kv_spec, kv_spec],
        out_specs=pl.BlockSpec((1, ds, D_MEM), lambda b: (b, 0, 0)),
        out_shape=jax.ShapeDtypeStruct((B, ds, D_MEM), BF16),
        compiler_params=_cparams(("parallel",)),
        name="mem_attn_sample",
    )(qm, mem_k, mem_v)


def _rwkv7_branch(p, shift_prev, s0, mu_shift, w0, w_lora_up, a0, a_lora_up, g_lora_up,
                  k_k, k_a, r_k, ln_x_w, ln_x_b):
    B, T, _ = p.shape
    prev = jnp.concatenate([shift_prev[:, None], p[:, :-1]], axis=1)
    xs = p + (prev - p) * mu_shift
    r, k, v, wd, ad, gd = jnp.split(
        xs, [D_RWKV, 2 * D_RWKV, 3 * D_RWKV, 3 * D_RWKV + DECAY_LORA,
             3 * D_RWKV + DECAY_LORA + AAA_LORA], axis=-1)
    w = -jax.nn.softplus(-(w0 + jnp.tanh(wd) @ w_lora_up)) - 0.5
    decay = jnp.exp(-jnp.exp(w))
    a = jax.nn.sigmoid(a0 + ad @ a_lora_up)
    g = jax.nn.sigmoid(gd) @ g_lora_up
    hd = lambda t: t.reshape(B, T, RW_HEADS, RW_HEAD)
    kk = hd(k * k_k)
    kk = kk / jnp.maximum(jnp.sqrt(jnp.sum(kk * kk, axis=-1, keepdims=True)), 1e-12)
    k_mod = k * (1.0 + (a - 1.0) * k_a)
    rh, kh, vh, ah, wh = hd(r), hd(k_mod), hd(v), hd(a), hd(decay)
    fl = lambda t: t.reshape(B, T, D_RWKV)
    y, s_fin = rwkv_scan(decay, r, k_mod, v, fl(kk), fl(kk * ah), s0)
    y = hd(y)
    mu = jnp.mean(y, axis=-1, keepdims=True)
    var = jnp.mean(jnp.square(y - mu), axis=-1, keepdims=True)
    yn = ((y - mu) * lax.rsqrt(var + GN_EPS)).reshape(B, T, D_RWKV)
    yn = yn * ln_x_w + ln_x_b
    bonus = (jnp.sum(rh * kh * r_k, axis=-1, keepdims=True) * vh).reshape(B, T, D_RWKV)
    out = (yn + bonus) * g
    return out, s_fin, p[:, -1]


C_K = D_ATT
C_V = 2 * D_ATT
C_QM = 3 * D_ATT
C_GATES = 3 * D_ATT + D_MEM


def _layer(x, shift_prev, s0, attend, mem_attend, W):
    B, T, _ = x.shape
    M = B * T
    x2 = x.reshape(M, D_MODEL)
    h = rmsnorm_bf16(x2, W['g_pre_mix'])
    p_a = matmul(h, W['w_in'], n_cols=RW_COLS_PAD)
    p_b = matmul(h, W['w_in_b'])
    p_rw = p_a[:, :RW_COLS].reshape(B, T, RW_COLS)
    k = p_b[:, C_K:C_V].reshape(B, T, ATT_HEADS, 2 * ATT_QK)
    v = p_b[:, C_V:C_QM].reshape(B, T, ATT_HEADS, ATT_V)

    o_rw, s_fin, shift_last = _rwkv7_branch(
        p_rw, shift_prev, s0, W['mu_shift'], W['w0'], W['w_lora_up'], W['a0'], W['a_lora_up'],
        W['g_lora_up'], W['k_k'], W['k_a'], W['r_k'], W['ln_x_w'], W['ln_x_b'])
    o_att = attend(p_b, k, v)
    o_mem = mem_attend(p_b)

    m = gated_merge(o_rw.reshape(M, D_RWKV).astype(BF16), o_att, o_mem,
                    W['w_br_rwkv'], W['w_br_attn'], W['w_br_mem'], p_b, C_GATES)
    z = matmul(m, W['w_out'])
    x1, h2 = norm_residual(z, W['g_post_mix'], x2, W['g_pre_mlp'])
    u = matmul(h2, W['w_up'], out_dtype=BF16, relu2=True)
    z2 = matmul(u, W['w_down'], tm=2048, tk=2048)
    y = norm_residual(z2, W['g_post_mlp'], x1)
    return y.reshape(B, T, D_MODEL), k, v, s_fin, shift_last


def kernel(x_prompt, x_sample, cache_k, cache_v, page_table, state_wkv, state_shift, cache_mem_k, cache_mem_v, mem_prompt, g_pre_mix, w_in, mu_shift, w0, w_lora_up, a0, a_lora_up, g_lora_up, k_k, k_a, r_k, ln_x_w, ln_x_b, lambda_q1, lambda_k1, lambda_q2, lambda_k2, g_subln, g_mem, w_mem_kv, w_br_rwkv, w_br_attn, w_br_mem, w_out, g_post_mix, g_pre_mlp, w_up, w_down, g_post_mlp):
    W = dict(g_pre_mix=g_pre_mix, w_in=w_in, w_in_b=w_in[:, RW_COLS:].astype(BF16),
             mu_shift=mu_shift, w0=w0, w_lora_up=w_lora_up,
             a0=a0, a_lora_up=a_lora_up, g_lora_up=g_lora_up, k_k=k_k, k_a=k_a, r_k=r_k,
             ln_x_w=ln_x_w, ln_x_b=ln_x_b, lambda_q1=lambda_q1, lambda_k1=lambda_k1,
             lambda_q2=lambda_q2, lambda_k2=lambda_k2, g_subln=g_subln, w_br_rwkv=w_br_rwkv,
             w_br_attn=w_br_attn, w_br_mem=w_br_mem, w_out=w_out, g_post_mix=g_post_mix,
             g_pre_mlp=g_pre_mlp, w_up=w_up, w_down=w_down, g_post_mlp=g_post_mlp)
    slopes = jnp.asarray([2.0 ** (-8.0 * (h + 1) / ATT_HEADS) for h in range(ATT_HEADS)], dtype=F32)
    lamp = jnp.stack([lambda_q1, lambda_k1, lambda_q2, lambda_k2])
    B, T, _ = x_prompt.shape
    BS, DS, _ = x_sample.shape
    MT = mem_prompt.shape[1]

    hm = rmsnorm_bf16(mem_prompt.reshape(B * MT, D_MODEL), g_mem)
    kv = matmul(hm, w_mem_kv)
    mem_k_prompt = kv[:, :D_MEM].reshape(B, MT, MEM_HEADS, MEM_HEAD)
    mem_v_prompt = kv[:, D_MEM:].reshape(B, MT, MEM_HEADS, MEM_HEAD)

    y_prompt, k_prompt, v_prompt, wkv_prompt, shift_prompt = _layer(
        x_prompt, jnp.zeros((B, RW_COLS), F32), jnp.zeros((B, RW_HEADS, RW_HEAD, RW_HEAD), F32),
        lambda p_b, k, v: diff_attn_prompt(p_b, B, T, slopes, lamp, g_subln),
        lambda p_b: mem_attn_prompt(p_b, kv, B, T, MT), W)

    def attend_sample(p_b, k, v):
        q = p_b[:, :C_K].reshape(BS, DS, D_ATT)
        return diff_attn_sample(q, k, v, cache_k, cache_v, page_table, slopes, lamp, g_subln).reshape(BS * DS, D_ATT)

    def mem_attend_sample(p_b):
        qm = p_b[:, C_QM:C_GATES].reshape(BS, DS, D_MEM)
        return mem_attn_sample(qm, cache_mem_k, cache_mem_v).reshape(BS * DS, D_MEM)

    y_sample, k_sample, v_sample, wkv_sample, shift_sample = _layer(
        x_sample, state_shift, state_wkv, attend_sample, mem_attend_sample, W)
    return (y_prompt, y_sample, k_prompt, v_prompt, k_sample, v_sample,
            wkv_prompt, wkv_sample, shift_prompt, shift_sample, mem_k_prompt, mem_v_prompt)
```

```python
import functools
import math

import jax
import jax.numpy as jnp
from jax import lax
from jax.experimental import pallas as pl
from jax.experimental.pallas import tpu as pltpu

F32 = jnp.float32
BF16 = jnp.bfloat16

D_MODEL = 4096
PAGE_SIZE = 128
D_RWKV = 2048
RW_HEAD = 64
RW_HEADS = 32
DECAY_LORA = 96
AAA_LORA = 96
GATE_LORA = 256
RW_COLS = 3 * D_RWKV + DECAY_LORA + AAA_LORA + GATE_LORA
RW_COLS_PAD = 6656
GN_EPS = 64e-5
D_ATT = 2048
ATT_HEADS = 8
ATT_QK = 128
ATT_V = 256
Q_BLOCK = 128
MEM_HEADS = 4
MEM_HEAD = 128
D_MEM = 512
D_FF = 4 * D_MODEL
NORM_EPS = 1e-6
NEG_INF = -1e30
LAM_INIT = 0.8 - 0.6 * math.exp(-0.3 * 0)

VMEM_LIMIT = 56 * 1024 * 1024
LANES = 128


def _cparams(sem):
    return pltpu.CompilerParams(dimension_semantics=sem, vmem_limit_bytes=VMEM_LIMIT)


def _rmsnorm_kernel(x_ref, g_ref, o_ref):
    x = x_ref[...]
    ms = jnp.mean(x * x, axis=-1, keepdims=True)
    o_ref[...] = (x * lax.rsqrt(ms + NORM_EPS) * g_ref[...]).astype(o_ref.dtype)


def rmsnorm_bf16(x, g, tr=256):
    m, d = x.shape
    tr = min(tr, m)
    return pl.pallas_call(
        _rmsnorm_kernel,
        grid=(m // tr,),
        in_specs=[pl.BlockSpec((tr, d), lambda i: (i, 0)),
                  pl.BlockSpec((1, d), lambda i: (0, 0))],
        out_specs=pl.BlockSpec((tr, d), lambda i: (i, 0)),
        out_shape=jax.ShapeDtypeStruct((m, d), BF16),
        compiler_params=_cparams(("parallel",)),
        name="rmsnorm_bf16",
    )(x, g.reshape(1, d))


def _mm_kernel(a_ref, b_ref, o_ref, *scratch, relu2, nk):
    def finish(r):
        if relu2:
            r = jnp.square(jnp.maximum(r, 0.0))
        o_ref[...] = r.astype(o_ref.dtype)

    part = jnp.dot(a_ref[...], b_ref[...].astype(BF16), preferred_element_type=F32)
    if nk == 1:
        finish(part)
        return
    acc_ref, = scratch
    k = pl.program_id(2)

    @pl.when(k == 0)
    def _():
        acc_ref[...] = part

    @pl.when(k > 0)
    def _():
        acc_ref[...] += part

    @pl.when(k == nk - 1)
    def _():
        finish(acc_ref[...])


def matmul(a, b, *, n_cols=None, tm=1024, tn=512, tk=4096, out_dtype=F32, relu2=False):
    m, kd = a.shape
    n = b.shape[1] if n_cols is None else n_cols
    tm = min(tm, m)
    tk = min(tk, kd)
    assert m % tm == 0 and n % tn == 0 and kd % tk == 0
    nk = kd // tk
    return pl.pallas_call(
        functools.partial(_mm_kernel, relu2=relu2, nk=nk),
        grid=(m // tm, n // tn, nk),
        in_specs=[pl.BlockSpec((tm, tk), lambda i, j, k: (i, k)),
                  pl.BlockSpec((tk, tn), lambda i, j, k: (k, j))],
        out_specs=pl.BlockSpec((tm, tn), lambda i, j, k: (i, j)),
        out_shape=jax.ShapeDtypeStruct((m, n), out_dtype),
        scratch_shapes=[pltpu.VMEM((tm, tn), F32)] if nk > 1 else [],
        compiler_params=_cparams(("parallel", "parallel", "arbitrary")),
        name="matmul",
    )(a, b)


def _norm_res_kernel(z_ref, g_ref, x_ref, *rest, with_next):
    z = z_ref[...]
    ms = jnp.mean(z * z, axis=-1, keepdims=True)
    y = x_ref[...] + z * lax.rsqrt(ms + NORM_EPS) * g_ref[...]
    if with_next:
        g2_ref, o_ref, h_ref = rest
        ms2 = jnp.mean(y * y, axis=-1, keepdims=True)
        h_ref[...] = (y * lax.rsqrt(ms2 + NORM_EPS) * g2_ref[...]).astype(h_ref.dtype)
    else:
        o_ref, = rest
    o_ref[...] = y


def norm_residual(z, g, x, g_next=None, tr=256):
    m, d = z.shape
    tr = min(tr, m)
    row = pl.BlockSpec((tr, d), lambda i: (i, 0))
    vec = pl.BlockSpec((1, d), lambda i: (0, 0))
    with_next = g_next is not None
    args = (z, g.reshape(1, d), x) + ((g_next.reshape(1, d),) if with_next else ())
    return pl.pallas_call(
        functools.partial(_norm_res_kernel, with_next=with_next),
        grid=(m // tr,),
        in_specs=[row, vec, row] + ([vec] if with_next else []),
        out_specs=[row, row] if with_next else row,
        out_shape=([jax.ShapeDtypeStruct((m, d), F32), jax.ShapeDtypeStruct((m, d), BF16)] if with_next
                   else jax.ShapeDtypeStruct((m, d), F32)),
        compiler_params=_cparams(("parallel",)),
        name="norm_residual",
    )(*args)


def _merge_kernel(orw_ref, oatt_ref, omem_ref, wr_ref, wa_ref, wm_ref, grw_ref, gatt_ref, gmem_ref, o_ref):
    def branch(o_ref_, w_ref_, g_ref_):
        y = jnp.dot(o_ref_[...], w_ref_[...].astype(BF16), preferred_element_type=F32)
        return jax.nn.sigmoid(g_ref_[...]) * y

    m = branch(orw_ref, wr_ref, grw_ref) + branch(oatt_ref, wa_ref, gatt_ref) + branch(omem_ref, wm_ref, gmem_ref)
    o_ref[...] = m.astype(o_ref.dtype)


def gated_merge(o_rw, o_att, o_mem, w_r, w_a, w_m, p_b, gate_col0, *, tm=512, tn=512):
    m = o_rw.shape[0]
    tm = min(tm, m)
    gb = gate_col0 // tn
    nb = D_MODEL // tn
    assert gate_col0 % tn == 0

    def gate_spec(b):
        return pl.BlockSpec((tm, tn), lambda i, j: (i, gb + b * nb + j))

    return pl.pallas_call(
        _merge_kernel,
        grid=(m // tm, D_MODEL // tn),
        in_specs=[pl.BlockSpec((tm, D_RWKV), lambda i, j: (i, 0)),
                  pl.BlockSpec((tm, D_ATT), lambda i, j: (i, 0)),
                  pl.BlockSpec((tm, D_MEM), lambda i, j: (i, 0)),
                  pl.BlockSpec((D_RWKV, tn), lambda i, j: (0, j)),
                  pl.BlockSpec((D_ATT, tn), lambda i, j: (0, j)),
                  pl.BlockSpec((D_MEM, tn), lambda i, j: (0, j)),
                  gate_spec(0), gate_spec(1), gate_spec(2)],
        out_specs=pl.BlockSpec((tm, tn), lambda i, j: (i, j)),
        out_shape=jax.ShapeDtypeStruct((m, D_MODEL), BF16),
        compiler_params=_cparams(("parallel", "parallel")),
        name="gated_merge",
    )(o_rw, o_att, o_mem, w_r, w_a, w_m, p_b, p_b, p_b)


RW_QUAD = 4 * RW_HEAD
RW_NQ = D_RWKV // RW_QUAD
RW_ROWS = RW_NQ * RW_HEAD
Y_TILE = 64


LORA_COLS = RW_COLS_PAD - 3 * D_RWKV


def _rwkv_pre_kernel(p_ref, prev_ref, mu_ref, w0_ref, a0_ref, kk_scale_ref, ka_ref, rk_ref,
                     wl_ref, al_ref, gl_ref, ones2_ref,
                     w_o, r_o, k_o, vh_o, vl_o, kk_o, b_o, bonus_o, g_o, *, boundary):
    p = p_ref[...]
    rows = p.shape[0]
    if boundary:
        row_id = lax.broadcasted_iota(jnp.int32, (rows, 1), 0)
        prev = jnp.where(row_id == 0, prev_ref[0:1, :], pltpu.roll(p, 1, 0))
    else:
        prev = prev_ref[...]
    xs = p + (prev - p) * mu_ref[...]
    r = xs[:, 0:D_RWKV]
    k = xs[:, D_RWKV:2 * D_RWKV]
    v = xs[:, 2 * D_RWKV:3 * D_RWKV]
    lo = xs[:, 3 * D_RWKV:]
    mm = lambda x, w_ref_: jnp.dot(x.astype(BF16), w_ref_[...], preferred_element_type=F32)
    w = -jax.nn.softplus(-(w0_ref[...] + mm(jnp.tanh(lo), wl_ref))) - 0.5
    a = jax.nn.sigmoid(a0_ref[...] + mm(lo, al_ref))
    def put(o_ref, q, x):
        o_ref[2 * q] = x[:, :LANES]
        o_ref[2 * q + 1] = x[:, LANES:]

    g_o[...] = mm(jax.nn.sigmoid(lo), gl_ref)
    decay = jnp.exp(-jnp.exp(w))
    k_mod = k * (1.0 + (a - 1.0) * ka_ref[...])
    kk = k * kk_scale_ref[...]
    rk = r * k_mod * rk_ref[...]
    lane = lax.broadcasted_iota(jnp.int32, (1, RW_QUAD), 1) % RW_HEAD
    for q in range(RW_NQ):
        sl = slice(q * RW_QUAD, (q + 1) * RW_QUAD)
        kq = kk[:, sl]
        norm = jnp.maximum(jnp.sqrt(_seg_sum(kq * kq, ones2_ref[...])), 1e-12)
        kq = kq / norm
        put(kk_o, q, kq)
        put(b_o, q, kq * a[:, sl])
        vq = v[:, sl]
        bonus_o[:, sl] = _seg_sum(rk[:, sl], ones2_ref[...]) * vq
        vh = vq.astype(BF16).astype(F32)
        vl = vq - vh
        put(vh_o, q, vh)
        half = RW_HEAD // 2
        put(vl_o, q, jnp.where(lane < half, pltpu.roll(vl, RW_QUAD - half, 1), pltpu.roll(vl, half, 1)))
        put(w_o, q, decay[:, sl])
        put(r_o, q, r[:, sl])
        put(k_o, q, k_mod[:, sl])


def rwkv_pre(p_a, prev, boundary, T, W, *, tr=128):
    m = p_a.shape[0]
    tr = min(tr, m)
    assert (not boundary or T % tr == 0) and m % tr == 0
    padc = lambda t: jnp.pad(t, (0, RW_COLS_PAD - RW_COLS)).reshape(1, RW_COLS_PAD)
    vec = lambda t: t.reshape(1, D_RWKV)

    def lora_pad(w_up, row0):
        return jnp.zeros((LORA_COLS, D_RWKV), BF16).at[row0:row0 + w_up.shape[0]].set(w_up.astype(BF16))

    ones = _block_diag_ones(BF16)
    ones2 = jnp.concatenate([ones, ones], axis=0)
    row = pl.BlockSpec((tr, RW_COLS_PAD), lambda i: (i, 0))
    prev_spec = pl.BlockSpec((8, RW_COLS_PAD), lambda i: (i, 0)) if boundary else row
    const = lambda shape: pl.BlockSpec(shape, lambda i: (0,) * len(shape))
    out = pl.BlockSpec((tr, D_RWKV), lambda i: (i, 0))
    nlb = D_RWKV // LANES
    out_lb = pl.BlockSpec((nlb, tr, LANES), lambda i: (0, i, 0))
    return pl.pallas_call(
        functools.partial(_rwkv_pre_kernel, boundary=boundary),
        grid=(m // tr,),
        in_specs=[row, prev_spec, const((1, RW_COLS_PAD))] + [const((1, D_RWKV))] * 5
                 + [const((LORA_COLS, D_RWKV))] * 3 + [const((2 * RW_QUAD, RW_QUAD))],
        out_specs=[out_lb] * 7 + [out] * 2,
        out_shape=[jax.ShapeDtypeStruct((nlb, m, LANES), F32)] * 7 + [jax.ShapeDtypeStruct((m, D_RWKV), F32)] * 2,
        compiler_params=_cparams(("parallel",)),
        name="rwkv_pre",
    )(p_a, prev, padc(W['mu_shift']), vec(W['w0']), vec(W['a0']), vec(W['k_k']), vec(W['k_a']), vec(W['r_k']),
      lora_pad(W['w_lora_up'], 0), lora_pad(W['a_lora_up'], DECAY_LORA),
      lora_pad(W['g_lora_up'], DECAY_LORA + AAA_LORA), ones2)


SCAN_BATCH = 4


def _block_diag_ones(dtype):
    r2 = lax.broadcasted_iota(jnp.int32, (RW_QUAD, RW_QUAD), 0) // RW_HEAD
    c2 = lax.broadcasted_iota(jnp.int32, (RW_QUAD, RW_QUAD), 1) // RW_HEAD
    return (r2 == c2).astype(dtype)


def _seg_sum(x, ones2):
    hi = x.astype(BF16)
    lo = (x - hi.astype(F32)).astype(BF16)
    return jnp.dot(jnp.concatenate([hi, lo], axis=1), ones2, preferred_element_type=F32)


XLANE_ROWS = RW_ROWS // 2


def _repeat_row(ref, lb, i, t):
    return ref[lb, i, pl.ds(t, RW_HEAD, stride=0), :]


def _seg_sum_xlane(x):
    first = lax.broadcasted_iota(jnp.int32, (1, LANES), 1) < RW_HEAD
    cols = []
    for c in range(x.shape[1] // LANES):
        xc = x[:, c * LANES:(c + 1) * LANES]
        sa = jnp.sum(jnp.where(first, xc, 0.0), axis=-1, keepdims=True)
        sb = jnp.sum(jnp.where(first, 0.0, xc), axis=-1, keepdims=True)
        cols.append(jnp.where(first, sa, sb))
    return jnp.concatenate(cols, axis=1)


def _scan_kernel(w_ref, r_ref, k_ref, vh_ref, vl_ref, kk_ref, b_ref, bonus_ref, g_ref, s0_ref,
                 eye_ref, eye2_ref, ones_ref, ones2_ref, lnw_ref, lnb_ref,
                 o_ref, s_ref, y_ref, nat_ref, *, tc, nb):
    c = pl.program_id(1)

    @pl.when(c == 0)
    def _():
        s_ref[...] = s0_ref[...]

    y_ref[...] = jnp.zeros(y_ref.shape, F32)
    lane_t = lax.broadcasted_iota(jnp.int32, (1, RW_QUAD), 1) % RW_HEAD

    def stack(ref, i, t):
        rep = lambda lb: _repeat_row(ref, lb, i, t)
        return jnp.concatenate(
            [jnp.concatenate([rep(2 * q), rep(2 * q + 1)], axis=1) for q in range(RW_NQ)], axis=0)

    def body(t, carry):
        for i in range(nb):
            s = s_ref[i]
            p = s * stack(kk_ref, i, t)
            s_kk = jnp.concatenate([_seg_sum_xlane(p[:XLANE_ROWS]), _seg_sum(p[XLANE_ROWS:], ones2_ref[...])], axis=0)
            v_pieces = eye_ref[...] * stack(vh_ref, i, t) + eye2_ref[...] * stack(vl_ref, i, t)
            v_col = jnp.dot(v_pieces.astype(BF16), ones_ref[...], preferred_element_type=F32)
            s = s * stack(w_ref, i, t) - s_kk * stack(b_ref, i, t) + v_col * stack(k_ref, i, t)
            s_ref[i] = s
            y_b = jnp.dot((s * stack(r_ref, i, t)).astype(BF16), ones_ref[...], preferred_element_type=F32)
            y_ref[i] = jnp.where(lane_t == t, y_b, y_ref[i])
        return carry

    lax.fori_loop(0, tc, body, 0)

    for i in range(nb):
        for q in range(RW_NQ):
            yq = y_ref[i, q * RW_HEAD:(q + 1) * RW_HEAD, :]
            mu = jnp.mean(yq, axis=0, keepdims=True)
            d = yq - mu
            var = jnp.mean(d * d, axis=0, keepdims=True)
            y_ref[i, q * RW_HEAD:(q + 1) * RW_HEAD, :] = d * lax.rsqrt(var + GN_EPS)
        yt = y_ref[i].T
        for q in range(RW_NQ):
            for h4 in range(4):
                ch = (q * 4 + h4) * RW_HEAD
                nat_ref[:, ch:ch + RW_HEAD] = yt[h4 * RW_HEAD:h4 * RW_HEAD + tc, q * RW_HEAD:(q + 1) * RW_HEAD]
        o = (nat_ref[...] * lnw_ref[...] + lnb_ref[...] + bonus_ref[i]) * g_ref[i]
        o_ref[i] = o.astype(o_ref.dtype)


def rwkv_scan(w, r, k, vh, vl, kk, b, bonus, g, s0, ln_w, ln_b):
    B, T, _ = bonus.shape
    tc = min(Y_TILE, T)
    nc = T // tc
    nb = SCAN_BATCH
    assert T % tc == 0 and B % nb == 0
    s0q = s0.reshape(B, RW_NQ, 4, RW_HEAD, RW_HEAD).transpose(0, 1, 3, 2, 4).reshape(B, RW_ROWS, RW_QUAD)
    rr = lax.broadcasted_iota(jnp.int32, (RW_ROWS, RW_QUAD), 0) % RW_HEAD
    cc = lax.broadcasted_iota(jnp.int32, (RW_ROWS, RW_QUAD), 1) % RW_HEAD
    eye = (rr == cc).astype(F32)
    eye2 = (rr == (cc + RW_HEAD // 2) % RW_HEAD).astype(F32)
    ones = _block_diag_ones(BF16)
    ones2 = jnp.concatenate([ones, ones], axis=0)
    xspec = pl.BlockSpec((nb, tc, D_RWKV), lambda bi, ci: (bi, ci, 0))
    sspec = pl.BlockSpec((nb, RW_ROWS, RW_QUAD), lambda bi, ci: (bi, 0, 0))
    const = lambda shape: pl.BlockSpec(shape, lambda bi, ci: (0,) * len(shape))
    o, s_fin = pl.pallas_call(
        functools.partial(_scan_kernel, tc=tc, nb=nb),
        grid=(B // nb, nc),
        in_specs=[pl.BlockSpec((D_RWKV // LANES, nb, tc, LANES), lambda bi, ci: (0, bi, ci, 0))] * 7
                 + [xspec] * 2 + [sspec, const((RW_ROWS, RW_QUAD)), const((RW_ROWS, RW_QUAD)),
                                const((RW_QUAD, RW_QUAD)), const((2 * RW_QUAD, RW_QUAD)),
                                const((1, D_RWKV)), const((1, D_RWKV))],
        out_specs=[xspec, sspec],
        out_shape=[jax.ShapeDtypeStruct((B, T, D_RWKV), BF16),
                   jax.ShapeDtypeStruct((B, RW_ROWS, RW_QUAD), F32)],
        scratch_shapes=[pltpu.VMEM((nb, RW_ROWS, RW_QUAD), F32), pltpu.VMEM((tc, D_RWKV), F32)],
        compiler_params=_cparams(("parallel", "arbitrary")),
        name="rwkv_scan",
    )(w, r, k, vh, vl, kk, b, bonus, g, s0q, eye, eye2, ones, ones2,
      ln_w.reshape(1, D_RWKV), ln_b.reshape(1, D_RWKV))
    s_fin = s_fin.reshape(B, RW_NQ, RW_HEAD, 4, RW_HEAD).transpose(0, 1, 3, 2, 4).reshape(B, RW_HEADS, RW_HEAD, RW_HEAD)
    return o, s_fin


ATT_SCALE = ATT_QK ** -0.5
DOT_NT = (((1,), (1,)), ((), ()))


def _lam_from(lamp_ref):
    lp = lamp_ref[...]
    return (jnp.exp(jnp.sum(lp[0:1] * lp[1:2])) - jnp.exp(jnp.sum(lp[2:3] * lp[3:4])) + LAM_INIT)


def _online_softmax_step(s, v_bf, m, l, a):
    m_new = jnp.maximum(m, jnp.max(s, axis=-1, keepdims=True))
    alpha = jnp.exp(m - m_new)
    p = jnp.exp(s - m_new)
    l = alpha * l + jnp.sum(p, axis=-1, keepdims=True)
    a = alpha * a + jnp.dot(p.astype(BF16), v_bf, preferred_element_type=F32)
    return m_new, l, a


def _subln(o, g):
    ms = jnp.mean(o * o, axis=-1, keepdims=True)
    return o * lax.rsqrt(ms + NORM_EPS) * g * (1.0 - LAM_INIT)


def _diff_prompt_kernel(slopes_ref, q_ref, k_ref, v_ref, lamp_ref, g_ref, o_ref, *, tq, tk):
    h = pl.program_id(1)
    qi = pl.program_id(2)
    slope = slopes_ref[h]
    lam = _lam_from(lamp_ref)
    q = q_ref[...]
    q1 = q[:, :ATT_QK].astype(BF16)
    q2 = q[:, ATT_QK:].astype(BF16)
    rel = (lax.broadcasted_iota(jnp.int32, (tq, tk), 0) - lax.broadcasted_iota(jnp.int32, (tq, tk), 1))
    base = -slope * rel.astype(F32)

    def kv_step(kb, carry, diagonal=False):
        m1, l1, a1, m2, l2, a2 = carry
        k0 = pl.multiple_of(kb * tk, tk)
        k = k_ref[pl.ds(k0, tk), :]
        v = v_ref[pl.ds(k0, tk), :].astype(BF16)
        bias = base - slope * (qi * tq - k0).astype(F32)
        if diagonal:
            bias = jnp.where(rel >= 0, bias, NEG_INF)
        s1 = lax.dot_general(q1, k[:, :ATT_QK].astype(BF16), DOT_NT, preferred_element_type=F32) * ATT_SCALE + bias
        s2 = lax.dot_general(q2, k[:, ATT_QK:].astype(BF16), DOT_NT, preferred_element_type=F32) * ATT_SCALE + bias
        m1, l1, a1 = _online_softmax_step(s1, v, m1, l1, a1)
        m2, l2, a2 = _online_softmax_step(s2, v, m2, l2, a2)
        return m1, l1, a1, m2, l2, a2

    col = lambda val: jnp.full((tq, 1), val, F32)
    acc0 = jnp.zeros((tq, ATT_V), F32)
    carry = lax.fori_loop(0, qi, kv_step, (col(NEG_INF), col(0.0), acc0, col(NEG_INF), col(0.0), acc0))
    m1, l1, a1, m2, l2, a2 = kv_step(qi, carry, diagonal=True)
    o = a1 / l1 - lam * (a2 / l2)
    o_ref[...] = _subln(o, g_ref[...]).astype(o_ref.dtype)


def diff_attn_prompt(p_b, B, T, slopes, lamp, g_subln, *, tq=512, tk=512):
    assert tq == tk and T % tq == 0
    nq = T // tq
    hb = D_ATT // ATT_V
    return pl.pallas_call(
        functools.partial(_diff_prompt_kernel, tq=tq, tk=tk),
        grid=(B, ATT_HEADS, nq),
        in_specs=[pl.BlockSpec(memory_space=pltpu.SMEM),
                  pl.BlockSpec((tq, ATT_V), lambda b, h, i: (b * nq + i, h)),
                  pl.BlockSpec((T, ATT_V), lambda b, h, i: (b, hb + h)),
                  pl.BlockSpec((T, ATT_V), lambda b, h, i: (b, 2 * hb + h)),
                  pl.BlockSpec((4, ATT_QK), lambda b, h, i: (0, 0)),
                  pl.BlockSpec((1, ATT_V), lambda b, h, i: (0, 0))],
        out_specs=pl.BlockSpec((tq, ATT_V), lambda b, h, i: (b * nq + i, h)),
        out_shape=jax.ShapeDtypeStruct((B * T, D_ATT), BF16),
        compiler_params=_cparams(("parallel", "parallel", "arbitrary")),
        name="diff_attn_prompt",
    )(slopes, p_b, p_b, p_b, lamp, g_subln.reshape(1, ATT_V))


PAGES_PER_STEP = 4
QROWS = 16


PAGE_ROWS = PAGE_SIZE * ATT_HEADS
DOT_TN = (((0,), (0,)), ((), ()))


def _diff_sample_kernel(pt_ref, slopes_ref, q_ref, kn_ref, vn_ref, lamp_ref, g_ref, *rest, ds, past):
    k_refs = rest[:PAGES_PER_STEP]
    v_refs = rest[PAGES_PER_STEP:2 * PAGES_PER_STEP]
    o_ref = rest[2 * PAGES_PER_STEP]
    qw_ref, bias_ref, m_ref, l_ref, a_ref = rest[2 * PAGES_PER_STEP + 1:]
    j = pl.program_id(1)
    cph = 2 * ds
    col = lax.broadcasted_iota(jnp.int32, (1, LANES), 1)
    col_head = col // cph
    col_q = col % ds
    slope_col = jnp.zeros((1, LANES), F32)
    for h in range(ATT_HEADS):
        slope_col = jnp.where(col_head == h, slopes_ref[h], slope_col)
    row_key = lax.broadcasted_iota(jnp.int32, (PAGE_ROWS, 1), 0) // ATT_HEADS
    row_head = lax.broadcasted_iota(jnp.int32, (PAGE_ROWS, 1), 0) % ATT_HEADS

    @pl.when(j == 0)
    def _():
        lane = lax.broadcasted_iota(jnp.int32, (ds, ATT_V), 1)
        rows = []
        for h in range(ATT_HEADS):
            qh = q_ref[0, :, h * ATT_V:(h + 1) * ATT_V]
            rows += [jnp.where(lane < ATT_QK, qh, 0.0), jnp.where(lane >= ATT_QK, qh, 0.0)]
        rows.append(jnp.zeros((LANES - ATT_HEADS * cph, ATT_V), F32))
        qw_ref[...] = jnp.concatenate(rows, axis=0).T.astype(BF16)
        bias_ref[...] = jnp.where(row_head == col_head, slope_col * row_key.astype(F32), NEG_INF)
        m_ref[...] = jnp.full(m_ref.shape, NEG_INF, F32)
        l_ref[...] = jnp.zeros(l_ref.shape, F32)
        a_ref[...] = jnp.zeros(a_ref.shape, F32)

    eye = (lax.broadcasted_iota(jnp.int32, (LANES, LANES), 0) == lax.broadcasted_iota(jnp.int32, (LANES, LANES), 1))
    to_col = lambda r: jnp.sum(jnp.where(eye, r, 0.0), axis=1, keepdims=True)

    def attend(k_ref, v_ref, key0, extra):
        k2 = k_ref[0].reshape(PAGE_ROWS, ATT_V).astype(BF16)
        v2 = v_ref[0].reshape(PAGE_ROWS, ATT_V).astype(BF16)
        s = jnp.dot(k2, qw_ref[...], preferred_element_type=F32) * ATT_SCALE + bias_ref[...]
        s = s + slope_col * (key0 - past - col_q).astype(F32)
        if extra is not None:
            s = jnp.where(extra, s, NEG_INF)
        m_old = m_ref[...]
        m_new = jnp.maximum(m_old, jnp.max(s, axis=0, keepdims=True))
        alpha = jnp.exp(m_old - m_new)
        p = jnp.exp(s - m_new)
        l_ref[...] = alpha * l_ref[...] + jnp.sum(p, axis=0, keepdims=True)
        m_ref[...] = m_new
        pv = lax.dot_general(p.astype(BF16), v2, DOT_TN, preferred_element_type=F32)
        a_ref[...] = a_ref[...] * to_col(alpha) + pv

    for i in range(PAGES_PER_STEP):
        attend(k_refs[i], v_refs[i], (j * PAGES_PER_STEP + i) * PAGE_SIZE, None)

    @pl.when(j == pl.num_programs(1) - 1)
    def _():
        attend(kn_ref, vn_ref, past, (row_key < ds) & (row_key <= col_q))
        lam = _lam_from(lamp_ref)
        o_all = a_ref[...] / to_col(l_ref[...])
        for h in range(ATT_HEADS):
            o = o_all[h * cph:h * cph + ds] - lam * o_all[h * cph + ds:(h + 1) * cph]
            o_ref[0, :, h * ATT_V:(h + 1) * ATT_V] = _subln(o, g_ref[...]).astype(o_ref.dtype)


def diff_attn_sample(q, k_new, v_new, cache_k, cache_v, page_table, slopes, lamp, g_subln):
    B, ds, _ = q.shape
    n_pages = page_table.shape[1]
    past = n_pages * PAGE_SIZE
    assert n_pages % PAGES_PER_STEP == 0 and ATT_HEADS * 2 * ds <= LANES
    pad = lambda t: jnp.pad(t, ((0, 0), (0, PAGE_SIZE - ds), (0, 0), (0, 0)))
    page_blk = (1, PAGE_SIZE, ATT_HEADS, ATT_V)

    def page_spec(i):
        return pl.BlockSpec(page_blk, lambda b, j, pt: (pt[b, j * PAGES_PER_STEP + i], 0, 0, 0))

    fixed = lambda shape: pl.BlockSpec(shape, lambda b, j, pt: (0,) * len(shape))
    grid_spec = pltpu.PrefetchScalarGridSpec(
        num_scalar_prefetch=1,
        grid=(B, n_pages // PAGES_PER_STEP),
        in_specs=[pl.BlockSpec(memory_space=pltpu.SMEM),
                  pl.BlockSpec((1, ds, D_ATT), lambda b, j, pt: (b, 0, 0)),
                  pl.BlockSpec(page_blk, lambda b, j, pt: (b, 0, 0, 0)),
                  pl.BlockSpec(page_blk, lambda b, j, pt: (b, 0, 0, 0)),
                  fixed((4, ATT_QK)), fixed((1, ATT_V))]
                 + [page_spec(i) for i in range(PAGES_PER_STEP)] * 2,
        out_specs=pl.BlockSpec((1, ds, D_ATT), lambda b, j, pt: (b, 0, 0)),
        scratch_shapes=[pltpu.VMEM((ATT_V, LANES), BF16),
                        pltpu.VMEM((PAGE_ROWS, LANES), F32),
                        pltpu.VMEM((1, LANES), F32),
                        pltpu.VMEM((1, LANES), F32),
                        pltpu.VMEM((LANES, ATT_V), F32)])
    return pl.pallas_call(
        functools.partial(_diff_sample_kernel, ds=ds, past=past),
        grid_spec=grid_spec,
        out_shape=jax.ShapeDtypeStruct((B, ds, D_ATT), BF16),
        compiler_params=_cparams(("parallel", "arbitrary")),
        name="diff_attn_sample",
    )(page_table, slopes, q, pad(k_new), pad(v_new), lamp, g_subln.reshape(1, ATT_V),
      *([cache_k] * PAGES_PER_STEP), *([cache_v] * PAGES_PER_STEP))


def _mem_attn_kernel(q_ref, k_ref, v_ref, o_ref, *, paged_heads, rows):
    for h in range(MEM_HEADS):
        sl = slice(h * MEM_HEAD, (h + 1) * MEM_HEAD)
        q = q_ref[:, sl] if q_ref.ndim == 2 else q_ref[0, :, sl]
        if q.shape[0] < rows:
            q = jnp.concatenate([q, jnp.zeros((rows - q.shape[0], MEM_HEAD), F32)], axis=0)
        k = k_ref[0, :, h, :] if paged_heads else k_ref[:, sl]
        v = v_ref[0, :, h, :] if paged_heads else v_ref[:, sl]
        s = lax.dot_general(q.astype(BF16), k.astype(BF16), DOT_NT, preferred_element_type=F32) * (MEM_HEAD ** -0.5)
        p = jnp.exp(s - jnp.max(s, axis=-1, keepdims=True))
        o = jnp.dot(p.astype(BF16), v.astype(BF16), preferred_element_type=F32) / jnp.sum(p, axis=-1, keepdims=True)
        if q_ref.ndim == 2:
            o_ref[:, sl] = o.astype(o_ref.dtype)
        else:
            o_ref[0, :, sl] = o[:q_ref.shape[1]].astype(o_ref.dtype)


def mem_attn_prompt(p_b, kv, B, T, MT, *, tq=512):
    tq = min(tq, T)
    nq = T // tq
    qcol = 3 * D_ATT // D_MEM
    return pl.pallas_call(
        functools.partial(_mem_attn_kernel, paged_heads=False, rows=tq),
        grid=(B, nq),
        in_specs=[pl.BlockSpec((tq, D_MEM), lambda b, i: (b * nq + i, qcol)),
                  pl.BlockSpec((MT, D_MEM), lambda b, i: (b, 0)),
                  pl.BlockSpec((MT, D_MEM), lambda b, i: (b, 1))],
        out_specs=pl.BlockSpec((tq, D_MEM), lambda b, i: (b * nq + i, 0)),
        out_shape=jax.ShapeDtypeStruct((B * T, D_MEM), BF16),
        compiler_params=_cparams(("parallel", "parallel")),
        name="mem_attn_prompt",
    )(p_b, kv, kv)


def mem_attn_sample(qm, mem_k, mem_v):
    B, ds, _ = qm.shape
    MT = mem_k.shape[1]
    kv_spec = pl.BlockSpec((1, MT, MEM_HEADS, MEM_HEAD), lambda b: (b, 0, 0, 0))
    return pl.pallas_call(
        functools.partial(_mem_attn_kernel, paged_heads=True, rows=QROWS),
        grid=(B,),
        in_specs=[pl.BlockSpec((1, ds, D_MEM), lambda b: (b, 0, 0)), kv_spec, kv_spec],
        out_specs=pl.BlockSpec((1, ds, D_MEM), lambda b: (b, 0, 0)),
        out_shape=jax.ShapeDtypeStruct((B, ds, D_MEM), BF16),
        compiler_params=_cparams(("parallel",)),
        name="mem_attn_sample",
    )(qm, mem_k, mem_v)


PRE_ROWS = 128


def _rwkv7_branch(p_a, B, T, shift_prev, s0, W):
    M = B * T
    sp = jnp.pad(shift_prev, ((0, 0), (0, RW_COLS_PAD - RW_COLS)))
    if T % PRE_ROWS == 0:
        first = jnp.arange(M // PRE_ROWS) * PRE_ROWS
        rows = jnp.where((first % T == 0)[:, None], sp[first // T], p_a[jnp.maximum(first - 1, 0)])
        prev = jnp.zeros((M // PRE_ROWS, 8, RW_COLS_PAD), F32).at[:, 0].set(rows).reshape(-1, RW_COLS_PAD)
        boundary = True
    else:
        p3 = p_a.reshape(B, T, RW_COLS_PAD)
        prev = jnp.concatenate([sp[:, None], p3[:, :-1]], axis=1).reshape(M, RW_COLS_PAD)
        boundary = False
    outs = rwkv_pre(p_a, prev, boundary, T, W, tr=PRE_ROWS)
    per_step = [t.reshape(D_RWKV // LANES, B, T, LANES) for t in outs[:7]]
    bonus, g = (t.reshape(B, T, D_RWKV) for t in outs[7:])
    o, s_fin = rwkv_scan(*per_step, bonus, g, s0, W['ln_x_w'], W['ln_x_b'])
    shift_last = p_a.reshape(B, T, RW_COLS_PAD)[:, -1, :RW_COLS]
    return o.reshape(M, D_RWKV), s_fin, shift_last


C_K = D_ATT
C_V = 2 * D_ATT
C_QM = 3 * D_ATT
C_GATES = 3 * D_ATT + D_MEM


def _layer(x, shift_prev, s0, attend, mem_attend, W):
    B, T, _ = x.shape
    M = B * T
    x2 = x.reshape(M, D_MODEL)
    h = rmsnorm_bf16(x2, W['g_pre_mix'])
    p_a = matmul(h, W['w_in'], n_cols=RW_COLS_PAD)
    p_b = matmul(h, W['w_in_b'])
    k = p_b[:, C_K:C_V].reshape(B, T, ATT_HEADS, 2 * ATT_QK)
    v = p_b[:, C_V:C_QM].reshape(B, T, ATT_HEADS, ATT_V)

    o_rw, s_fin, shift_last = _rwkv7_branch(p_a, B, T, shift_prev, s0, W)
    o_att = attend(p_b, k, v)
    o_mem = mem_attend(p_b)

    m = gated_merge(o_rw, o_att, o_mem,
                    W['w_br_rwkv'], W['w_br_attn'], W['w_br_mem'], p_b, C_GATES)
    z = matmul(m, W['w_out'])
    x1, h2 = norm_residual(z, W['g_post_mix'], x2, W['g_pre_mlp'])
    u = matmul(h2, W['w_up'], out_dtype=BF16, relu2=True)
    z2 = matmul(u, W['w_down'], tm=2048, tk=2048)
    y = norm_residual(z2, W['g_post_mlp'], x1)
    return y.reshape(B, T, D_MODEL), k, v, s_fin, shift_last


def kernel(x_prompt, x_sample, cache_k, cache_v, page_table, state_wkv, state_shift, cache_mem_k, cache_mem_v, mem_prompt, g_pre_mix, w_in, mu_shift, w0, w_lora_up, a0, a_lora_up, g_lora_up, k_k, k_a, r_k, ln_x_w, ln_x_b, lambda_q1, lambda_k1, lambda_q2, lambda_k2, g_subln, g_mem, w_mem_kv, w_br_rwkv, w_br_attn, w_br_mem, w_out, g_post_mix, g_pre_mlp, w_up, w_down, g_post_mlp):
    W = dict(g_pre_mix=g_pre_mix, w_in=w_in, w_in_b=w_in[:, RW_COLS:].astype(BF16),
             mu_shift=mu_shift, w0=w0, w_lora_up=w_lora_up,
             a0=a0, a_lora_up=a_lora_up, g_lora_up=g_lora_up, k_k=k_k, k_a=k_a, r_k=r_k,
             ln_x_w=ln_x_w, ln_x_b=ln_x_b, lambda_q1=lambda_q1, lambda_k1=lambda_k1,
             lambda_q2=lambda_q2, lambda_k2=lambda_k2, g_subln=g_subln, w_br_rwkv=w_br_rwkv,
             w_br_attn=w_br_attn, w_br_mem=w_br_mem, w_out=w_out, g_post_mix=g_post_mix,
             g_pre_mlp=g_pre_mlp, w_up=w_up, w_down=w_down, g_post_mlp=g_post_mlp)
    slopes = jnp.asarray([2.0 ** (-8.0 * (h + 1) / ATT_HEADS) for h in range(ATT_HEADS)], dtype=F32)
    lamp = jnp.stack([lambda_q1, lambda_k1, lambda_q2, lambda_k2])
    B, T, _ = x_prompt.shape
    BS, DS, _ = x_sample.shape
    MT = mem_prompt.shape[1]

    hm = rmsnorm_bf16(mem_prompt.reshape(B * MT, D_MODEL), g_mem)
    kv = matmul(hm, w_mem_kv)
    mem_k_prompt = kv[:, :D_MEM].reshape(B, MT, MEM_HEADS, MEM_HEAD)
    mem_v_prompt = kv[:, D_MEM:].reshape(B, MT, MEM_HEADS, MEM_HEAD)

    y_prompt, k_prompt, v_prompt, wkv_prompt, shift_prompt = _layer(
        x_prompt, jnp.zeros((B, RW_COLS), F32), jnp.zeros((B, RW_HEADS, RW_HEAD, RW_HEAD), F32),
        lambda p_b, k, v: diff_attn_prompt(p_b, B, T, slopes, lamp, g_subln),
        lambda p_b: mem_attn_prompt(p_b, kv, B, T, MT), W)

    def attend_sample(p_b, k, v):
        q = p_b[:, :C_K].reshape(BS, DS, D_ATT)
        return diff_attn_sample(q, k, v, cache_k, cache_v, page_table, slopes, lamp, g_subln).reshape(BS * DS, D_ATT)

    def mem_attend_sample(p_b):
        qm = p_b[:, C_QM:C_GATES].reshape(BS, DS, D_MEM)
        return mem_attn_sample(qm, cache_mem_k, cache_mem_v).reshape(BS * DS, D_MEM)

    y_sample, k_sample, v_sample, wkv_sample, shift_sample = _layer(
        x_sample, state_shift, state_wkv, attend_sample, mem_attend_sample, W)
    return (y_prompt, y_sample, k_prompt, v_prompt, k_sample, v_sample,
            wkv_prompt, wkv_sample, shift_prompt, shift_sample, mem_k_prompt, mem_v_prompt)
```

```python
import functools
import math

import jax
import jax.numpy as jnp
from jax import lax
from jax.experimental import pallas as pl
from jax.experimental.pallas import tpu as pltpu

F32 = jnp.float32
BF16 = jnp.bfloat16

D_MODEL = 4096
PAGE_SIZE = 128
D_RWKV = 2048
RW_HEAD = 64
RW_HEADS = 32
DECAY_LORA = 96
AAA_LORA = 96
GATE_LORA = 256
RW_COLS = 3 * D_RWKV + DECAY_LORA + AAA_LORA + GATE_LORA
RW_COLS_PAD = 6656
GN_EPS = 64e-5
D_ATT = 2048
ATT_HEADS = 8
ATT_QK = 128
ATT_V = 256
Q_BLOCK = 128
MEM_HEADS = 4
MEM_HEAD = 128
D_MEM = 512
D_FF = 4 * D_MODEL
NORM_EPS = 1e-6
NEG_INF = -1e30
LAM_INIT = 0.8 - 0.6 * math.exp(-0.3 * 0)

VMEM_LIMIT = 56 * 1024 * 1024
LANES = 128


def _cparams(sem):
    return pltpu.CompilerParams(dimension_semantics=sem, vmem_limit_bytes=VMEM_LIMIT)


def _rmsnorm_kernel(x_ref, g_ref, o_ref):
    x = x_ref[...]
    ms = jnp.mean(x * x, axis=-1, keepdims=True)
    o_ref[...] = (x * lax.rsqrt(ms + NORM_EPS) * g_ref[...]).astype(o_ref.dtype)


def rmsnorm_bf16(x, g, tr=256):
    m, d = x.shape
    tr = min(tr, m)
    return pl.pallas_call(
        _rmsnorm_kernel,
        grid=(m // tr,),
        in_specs=[pl.BlockSpec((tr, d), lambda i: (i, 0)),
                  pl.BlockSpec((1, d), lambda i: (0, 0))],
        out_specs=pl.BlockSpec((tr, d), lambda i: (i, 0)),
        out_shape=jax.ShapeDtypeStruct((m, d), BF16),
        compiler_params=_cparams(("parallel",)),
        name="rmsnorm_bf16",
    )(x, g.reshape(1, d))


def _mm_kernel(a_ref, b_ref, o_ref, *scratch, relu2, nk):
    def finish(r):
        if relu2:
            r = jnp.square(jnp.maximum(r, 0.0))
        o_ref[...] = r.astype(o_ref.dtype)

    part = jnp.dot(a_ref[...], b_ref[...].astype(BF16), preferred_element_type=F32)
    if nk == 1:
        finish(part)
        return
    acc_ref, = scratch
    k = pl.program_id(2)

    @pl.when(k == 0)
    def _():
        acc_ref[...] = part

    @pl.when(k > 0)
    def _():
        acc_ref[...] += part

    @pl.when(k == nk - 1)
    def _():
        finish(acc_ref[...])


def matmul(a, b, *, col0=0, n_cols=None, tm=1024, tn=512, tk=4096, out_dtype=F32, relu2=False):
    m, kd = a.shape
    n = b.shape[1] - col0 if n_cols is None else n_cols
    tm = min(tm, m)
    tk = min(tk, kd)
    assert m % tm == 0 and n % tn == 0 and kd % tk == 0 and col0 % tn == 0
    nk = kd // tk
    jb = col0 // tn
    return pl.pallas_call(
        functools.partial(_mm_kernel, relu2=relu2, nk=nk),
        grid=(m // tm, n // tn, nk),
        in_specs=[pl.BlockSpec((tm, tk), lambda i, j, k: (i, k)),
                  pl.BlockSpec((tk, tn), lambda i, j, k: (k, j + jb))],
        out_specs=pl.BlockSpec((tm, tn), lambda i, j, k: (i, j)),
        out_shape=jax.ShapeDtypeStruct((m, n), out_dtype),
        scratch_shapes=[pltpu.VMEM((tm, tn), F32)] if nk > 1 else [],
        compiler_params=_cparams(("parallel", "parallel", "arbitrary")),
        name="matmul",
    )(a, b)


def _norm_res_kernel(z_ref, g_ref, x_ref, *rest, with_next):
    z = z_ref[...]
    ms = jnp.mean(z * z, axis=-1, keepdims=True)
    y = x_ref[...] + z * lax.rsqrt(ms + NORM_EPS) * g_ref[...]
    if with_next:
        g2_ref, o_ref, h_ref = rest
        ms2 = jnp.mean(y * y, axis=-1, keepdims=True)
        h_ref[...] = (y * lax.rsqrt(ms2 + NORM_EPS) * g2_ref[...]).astype(h_ref.dtype)
    else:
        o_ref, = rest
    o_ref[...] = y


def norm_residual(z, g, x, g_next=None, tr=256):
    m, d = z.shape
    tr = min(tr, m)
    row = pl.BlockSpec((tr, d), lambda i: (i, 0))
    vec = pl.BlockSpec((1, d), lambda i: (0, 0))
    with_next = g_next is not None
    args = (z, g.reshape(1, d), x) + ((g_next.reshape(1, d),) if with_next else ())
    return pl.pallas_call(
        functools.partial(_norm_res_kernel, with_next=with_next),
        grid=(m // tr,),
        in_specs=[row, vec, row] + ([vec] if with_next else []),
        out_specs=[row, row] if with_next else row,
        out_shape=([jax.ShapeDtypeStruct((m, d), F32), jax.ShapeDtypeStruct((m, d), BF16)] if with_next
                   else jax.ShapeDtypeStruct((m, d), F32)),
        compiler_params=_cparams(("parallel",)),
        name="norm_residual",
    )(*args)


def _merge_kernel(orw_ref, oatt_ref, omem_ref, wr_ref, wa_ref, wm_ref, grw_ref, gatt_ref, gmem_ref, o_ref):
    def branch(o_ref_, w_ref_, g_ref_):
        y = jnp.dot(o_ref_[...], w_ref_[...], preferred_element_type=F32)
        return jax.nn.sigmoid(g_ref_[...]) * y

    m = branch(orw_ref, wr_ref, grw_ref) + branch(oatt_ref, wa_ref, gatt_ref) + branch(omem_ref, wm_ref, gmem_ref)
    o_ref[...] = m.astype(o_ref.dtype)


def gated_merge(o_rw, o_att, o_mem, w_r, w_a, w_m, p_b, gate_col0, *, tm=1024, tn=512):
    m = o_rw.shape[0]
    tm = min(tm, m)
    gb = gate_col0 // tn
    nb = D_MODEL // tn
    assert gate_col0 % tn == 0

    def gate_spec(b):
        return pl.BlockSpec((tm, tn), lambda i, j: (i, gb + b * nb + j))

    return pl.pallas_call(
        _merge_kernel,
        grid=(m // tm, D_MODEL // tn),
        in_specs=[pl.BlockSpec((tm, D_RWKV), lambda i, j: (i, 0)),
                  pl.BlockSpec((tm, D_ATT), lambda i, j: (i, 0)),
                  pl.BlockSpec((tm, D_MEM), lambda i, j: (i, 0)),
                  pl.BlockSpec((D_RWKV, tn), lambda i, j: (0, j)),
                  pl.BlockSpec((D_ATT, tn), lambda i, j: (0, j)),
                  pl.BlockSpec((D_MEM, tn), lambda i, j: (0, j)),
                  gate_spec(0), gate_spec(1), gate_spec(2)],
        out_specs=pl.BlockSpec((tm, tn), lambda i, j: (i, j)),
        out_shape=jax.ShapeDtypeStruct((m, D_MODEL), BF16),
        compiler_params=_cparams(("parallel", "parallel")),
        name="gated_merge",
    )(o_rw, o_att, o_mem, w_r, w_a, w_m, p_b, p_b, p_b)


RW_QUAD = 4 * RW_HEAD
RW_NQ = D_RWKV // RW_QUAD
RW_ROWS = RW_NQ * RW_HEAD
Y_TILE = 64


LORA_COLS = RW_COLS_PAD - 3 * D_RWKV


def _rwkv_pre_kernel(p_ref, prev_ref, mu_ref, w0_ref, a0_ref, kk_scale_ref, ka_ref, rk_ref,
                     wl_ref, al_ref, gl_ref, ones2_ref,
                     w_o, r_o, k_o, kk_o, b_o, v_o, bonus_o, g_o, *, boundary):
    p = p_ref[...]
    rows = p.shape[0]
    if boundary:
        row_id = lax.broadcasted_iota(jnp.int32, (rows, 1), 0)
        prev = jnp.where(row_id == 0, prev_ref[0:1, :], pltpu.roll(p, 1, 0))
    else:
        prev = prev_ref[...]
    xs = p + (prev - p) * mu_ref[...]
    r = xs[:, 0:D_RWKV]
    k = xs[:, D_RWKV:2 * D_RWKV]
    v = xs[:, 2 * D_RWKV:3 * D_RWKV]
    lo = xs[:, 3 * D_RWKV:]
    mm = lambda x, w_ref_: jnp.dot(x.astype(BF16), w_ref_[...], preferred_element_type=F32)
    w = -jax.nn.softplus(-(w0_ref[...] + mm(jnp.tanh(lo), wl_ref))) - 0.5
    a = jax.nn.sigmoid(a0_ref[...] + mm(lo, al_ref))
    def put(o_ref, q, x):
        o_ref[2 * q] = x[:, :LANES]
        o_ref[2 * q + 1] = x[:, LANES:]

    g_o[...] = mm(jax.nn.sigmoid(lo), gl_ref)
    decay = jnp.exp(-jnp.exp(w))
    k_mod = k * (1.0 + (a - 1.0) * ka_ref[...])
    kk = k * kk_scale_ref[...]
    rk = r * k_mod * rk_ref[...]
    v_o[...] = v
    for q in range(RW_NQ):
        sl = slice(q * RW_QUAD, (q + 1) * RW_QUAD)
        kq = kk[:, sl]
        norm = jnp.maximum(jnp.sqrt(_seg_sum(kq * kq, ones2_ref[...])), 1e-12)
        kq = kq / norm
        put(kk_o, q, kq)
        put(b_o, q, kq * a[:, sl])
        vq = v[:, sl]
        bonus_o[:, sl] = _seg_sum(rk[:, sl], ones2_ref[...]) * vq
        put(w_o, q, decay[:, sl])
        put(r_o, q, r[:, sl])
        put(k_o, q, k_mod[:, sl])


def rwkv_pre(p_a, prev, boundary, T, W, *, tr=128):
    m = p_a.shape[0]
    tr = min(tr, m)
    assert (not boundary or T % tr == 0) and m % tr == 0
    padc = lambda t: jnp.pad(t, (0, RW_COLS_PAD - RW_COLS)).reshape(1, RW_COLS_PAD)
    vec = lambda t: t.reshape(1, D_RWKV)

    def lora_pad(w_up, row0):
        return jnp.zeros((LORA_COLS, D_RWKV), BF16).at[row0:row0 + w_up.shape[0]].set(w_up.astype(BF16))

    ones = _block_diag_ones(BF16)
    ones2 = jnp.concatenate([ones, ones], axis=0)
    row = pl.BlockSpec((tr, RW_COLS_PAD), lambda i: (i, 0))
    prev_spec = pl.BlockSpec((8, RW_COLS_PAD), lambda i: (i, 0)) if boundary else row
    const = lambda shape: pl.BlockSpec(shape, lambda i: (0,) * len(shape))
    out = pl.BlockSpec((tr, D_RWKV), lambda i: (i, 0))
    nlb = D_RWKV // LANES
    out_lb = pl.BlockSpec((nlb, tr, LANES), lambda i: (0, i, 0))
    return pl.pallas_call(
        functools.partial(_rwkv_pre_kernel, boundary=boundary),
        grid=(m // tr,),
        in_specs=[row, prev_spec, const((1, RW_COLS_PAD))] + [const((1, D_RWKV))] * 5
                 + [const((LORA_COLS, D_RWKV))] * 3 + [const((2 * RW_QUAD, RW_QUAD))],
        out_specs=[out_lb] * 5 + [out] * 3,
        out_shape=[jax.ShapeDtypeStruct((nlb, m, LANES), F32)] * 5 + [jax.ShapeDtypeStruct((m, D_RWKV), F32)] * 3,
        compiler_params=_cparams(("parallel",)),
        name="rwkv_pre",
    )(p_a, prev, padc(W['mu_shift']), vec(W['w0']), vec(W['a0']), vec(W['k_k']), vec(W['k_a']), vec(W['r_k']),
      lora_pad(W['w_lora_up'], 0), lora_pad(W['a_lora_up'], DECAY_LORA),
      lora_pad(W['g_lora_up'], DECAY_LORA + AAA_LORA), ones2)


SCAN_BATCH = 4


def _block_diag_ones(dtype):
    r2 = lax.broadcasted_iota(jnp.int32, (RW_QUAD, RW_QUAD), 0) // RW_HEAD
    c2 = lax.broadcasted_iota(jnp.int32, (RW_QUAD, RW_QUAD), 1) // RW_HEAD
    return (r2 == c2).astype(dtype)


def _seg_sum(x, ones2):
    hi = x.astype(BF16)
    lo = (x - hi.astype(F32)).astype(BF16)
    return jnp.dot(jnp.concatenate([hi, lo], axis=1), ones2, preferred_element_type=F32)


def _repeat_row(ref, lb, i, t):
    return ref[lb, i, pl.ds(t, RW_HEAD, stride=0), :]


def _seg_sum_xlane(x):
    first = lax.broadcasted_iota(jnp.int32, (1, LANES), 1) < RW_HEAD
    cols = []
    for c in range(x.shape[1] // LANES):
        xc = x[:, c * LANES:(c + 1) * LANES]
        sa = jnp.sum(jnp.where(first, xc, 0.0), axis=-1, keepdims=True)
        sb = jnp.sum(jnp.where(first, 0.0, xc), axis=-1, keepdims=True)
        cols.append(jnp.where(first, sa, sb))
    return jnp.concatenate(cols, axis=1)


V_HALF = 32


def _scan_kernel(w_ref, r_ref, k_ref, kk_ref, b_ref, v_ref, bonus_ref, g_ref, s0_ref,
                 tsel_ref, ones_ref, ones2_ref, lnw_ref, lnb_ref,
                 o_ref, s_ref, y_ref, nat_ref, vt_ref, vtmp_ref, *, tc, nb):
    c = pl.program_id(1)
    n_half = (tc + V_HALF - 1) // V_HALF

    @pl.when(c == 0)
    def _():
        s_ref[...] = s0_ref[...]

    y_ref[...] = jnp.zeros(y_ref.shape, F32)
    lane_t = lax.broadcasted_iota(jnp.int32, (1, RW_QUAD), 1) % RW_HEAD

    for i in range(nb):
        v = v_ref[i]
        if tc < LANES:
            v = jnp.concatenate([v, jnp.zeros((LANES - tc, D_RWKV), F32)], axis=0)
        vt = v.T
        hi = vt.astype(BF16).astype(F32)
        pieces = (hi, vt - hi)
        for a in range(n_half):
            for q in range(RW_NQ):
                for pc in range(2):
                    for h4 in range(4):
                        src = pieces[pc][(q * 4 + h4) * RW_HEAD:(q * 4 + h4 + 1) * RW_HEAD, a * V_HALF:(a + 1) * V_HALF]
                        dst = (pc * 4 + h4) * V_HALF
                        vtmp_ref[q * RW_HEAD:(q + 1) * RW_HEAD, dst:dst + V_HALF] = src
            vt_ref[a, i * RW_ROWS:(i + 1) * RW_ROWS, :] = vtmp_ref[...].astype(BF16)

    def stack(ref, i, t):
        rep = lambda lb: _repeat_row(ref, lb, i, t)
        return jnp.concatenate(
            [jnp.concatenate([rep(2 * q), rep(2 * q + 1)], axis=1) for q in range(RW_NQ)], axis=0)

    def body(t, carry, a):
        sel = jnp.where(tsel_ref[...] == t - a * V_HALF, 1.0, 0.0).astype(BF16)
        rows = lambda ref: jnp.concatenate([stack(ref, i, t) for i in range(nb)], axis=0)
        s = jnp.concatenate([s_ref[i] for i in range(nb)], axis=0)
        p = s * rows(kk_ref)
        xl = nb * RW_ROWS // 2
        s_kk = jnp.concatenate([_seg_sum_xlane(p[:xl]), _seg_sum(p[xl:], ones2_ref[...])], axis=0)
        v_col = jnp.dot(vt_ref[a], sel, preferred_element_type=F32)
        s = s * rows(w_ref) - s_kk * rows(b_ref) + v_col * rows(k_ref)
        for i in range(nb):
            s_ref[i] = s[i * RW_ROWS:(i + 1) * RW_ROWS]
        y_b = jnp.dot((s * rows(r_ref)).astype(BF16), ones_ref[...], preferred_element_type=F32)
        y_ref[...] = jnp.where(lane_t == t, y_b, y_ref[...])
        return carry

    for a in range(n_half):
        lax.fori_loop(a * V_HALF, min(tc, (a + 1) * V_HALF), functools.partial(body, a=a), 0)

    for i in range(nb):
        for q in range(RW_NQ):
            rq = slice(i * RW_ROWS + q * RW_HEAD, i * RW_ROWS + (q + 1) * RW_HEAD)
            yq = y_ref[rq, :]
            mu = jnp.mean(yq, axis=0, keepdims=True)
            d = yq - mu
            var = jnp.mean(d * d, axis=0, keepdims=True)
            y_ref[rq, :] = d * lax.rsqrt(var + GN_EPS)
        yt = y_ref[i * RW_ROWS:(i + 1) * RW_ROWS, :].T
        for q in range(RW_NQ):
            for h4 in range(4):
                ch = (q * 4 + h4) * RW_HEAD
                nat_ref[:, ch:ch + RW_HEAD] = yt[h4 * RW_HEAD:h4 * RW_HEAD + tc, q * RW_HEAD:(q + 1) * RW_HEAD]
        o = (nat_ref[...] * lnw_ref[...] + lnb_ref[...] + bonus_ref[i]) * g_ref[i]
        o_ref[i] = o.astype(o_ref.dtype)


def rwkv_scan(w, r, k, kk, b, v, bonus, g, s0, ln_w, ln_b):
    B, T, _ = bonus.shape
    tc = min(Y_TILE, T)
    nc = T // tc
    nb = SCAN_BATCH
    assert T % tc == 0 and B % nb == 0
    s0q = s0.reshape(B, RW_NQ, 4, RW_HEAD, RW_HEAD).transpose(0, 1, 3, 2, 4).reshape(B, RW_ROWS, RW_QUAD)
    rr = lax.broadcasted_iota(jnp.int32, (RW_QUAD, RW_QUAD), 0)
    cc = lax.broadcasted_iota(jnp.int32, (RW_QUAD, RW_QUAD), 1)
    tsel = jnp.where((rr // V_HALF) % 4 == cc // RW_HEAD, rr % V_HALF, -1)
    ones = _block_diag_ones(BF16)
    ones2 = jnp.concatenate([ones, ones], axis=0)
    xspec = pl.BlockSpec((nb, tc, D_RWKV), lambda bi, ci: (bi, ci, 0))
    sspec = pl.BlockSpec((nb, RW_ROWS, RW_QUAD), lambda bi, ci: (bi, 0, 0))
    const = lambda shape: pl.BlockSpec(shape, lambda bi, ci: (0,) * len(shape))
    o, s_fin = pl.pallas_call(
        functools.partial(_scan_kernel, tc=tc, nb=nb),
        grid=(B // nb, nc),
        in_specs=[pl.BlockSpec((D_RWKV // LANES, nb, tc, LANES), lambda bi, ci: (0, bi, ci, 0))] * 5
                 + [xspec] * 3 + [sspec, const((RW_QUAD, RW_QUAD)),
                                const((RW_QUAD, RW_QUAD)), const((2 * RW_QUAD, RW_QUAD)),
                                const((1, D_RWKV)), const((1, D_RWKV))],
        out_specs=[xspec, sspec],
        out_shape=[jax.ShapeDtypeStruct((B, T, D_RWKV), BF16),
                   jax.ShapeDtypeStruct((B, RW_ROWS, RW_QUAD), F32)],
        scratch_shapes=[pltpu.VMEM((nb * RW_ROWS, RW_QUAD), F32), pltpu.VMEM((tc, D_RWKV), F32),
                        pltpu.VMEM(((tc + V_HALF - 1) // V_HALF, nb * RW_ROWS, RW_QUAD), BF16),
                        pltpu.VMEM((RW_ROWS, RW_QUAD), F32)],
        compiler_params=_cparams(("parallel", "arbitrary")),
        name="rwkv_scan",
    )(w, r, k, kk, b, v, bonus, g, s0q, tsel, ones, ones2,
      ln_w.reshape(1, D_RWKV), ln_b.reshape(1, D_RWKV))
    s_fin = s_fin.reshape(B, RW_NQ, RW_HEAD, 4, RW_HEAD).transpose(0, 1, 3, 2, 4).reshape(B, RW_HEADS, RW_HEAD, RW_HEAD)
    return o, s_fin


ATT_SCALE = ATT_QK ** -0.5
DOT_NT = (((1,), (1,)), ((), ()))


def _lam_from(lamp_ref):
    lp = lamp_ref[...]
    return (jnp.exp(jnp.sum(lp[0:1] * lp[1:2])) - jnp.exp(jnp.sum(lp[2:3] * lp[3:4])) + LAM_INIT)


def _online_softmax_step(s, v_bf, m, l, a):
    m_new = jnp.maximum(m, jnp.max(s, axis=-1, keepdims=True))
    alpha = jnp.exp(m - m_new)
    p = jnp.exp(s - m_new)
    l = alpha * l + jnp.sum(p, axis=-1, keepdims=True)
    a = alpha * a + jnp.dot(p.astype(BF16), v_bf, preferred_element_type=F32)
    return m_new, l, a


def _subln(o, g):
    ms = jnp.mean(o * o, axis=-1, keepdims=True)
    return o * lax.rsqrt(ms + NORM_EPS) * g * (1.0 - LAM_INIT)


def _diff_prompt_kernel(slopes_ref, q_ref, k_ref, v_ref, lamp_ref, g_ref, o_ref, *, tq, tk):
    h = pl.program_id(1)
    qi = pl.program_id(2)
    slope = slopes_ref[h]
    lam = _lam_from(lamp_ref)
    q = q_ref[...]
    q1 = q[:, :ATT_QK].astype(BF16)
    q2 = q[:, ATT_QK:].astype(BF16)
    rel = (lax.broadcasted_iota(jnp.int32, (tq, tk), 0) - lax.broadcasted_iota(jnp.int32, (tq, tk), 1))
    base = -slope * rel.astype(F32)

    def kv_step(kb, carry, diagonal=False):
        m1, l1, a1, m2, l2, a2 = carry
        k0 = pl.multiple_of(kb * tk, tk)
        k = k_ref[pl.ds(k0, tk), :]
        v = v_ref[pl.ds(k0, tk), :].astype(BF16)
        bias = base - slope * (qi * tq - k0).astype(F32)
        if diagonal:
            bias = jnp.where(rel >= 0, bias, NEG_INF)
        s1 = lax.dot_general(q1, k[:, :ATT_QK].astype(BF16), DOT_NT, preferred_element_type=F32) * ATT_SCALE + bias
        s2 = lax.dot_general(q2, k[:, ATT_QK:].astype(BF16), DOT_NT, preferred_element_type=F32) * ATT_SCALE + bias
        m1, l1, a1 = _online_softmax_step(s1, v, m1, l1, a1)
        m2, l2, a2 = _online_softmax_step(s2, v, m2, l2, a2)
        return m1, l1, a1, m2, l2, a2

    col = lambda val: jnp.full((tq, 1), val, F32)
    acc0 = jnp.zeros((tq, ATT_V), F32)
    carry = lax.fori_loop(0, qi, kv_step, (col(NEG_INF), col(0.0), acc0, col(NEG_INF), col(0.0), acc0))
    m1, l1, a1, m2, l2, a2 = kv_step(qi, carry, diagonal=True)
    o = a1 / l1 - lam * (a2 / l2)
    o_ref[...] = _subln(o, g_ref[...]).astype(o_ref.dtype)


def diff_attn_prompt(q, k, v, B, T, slopes, lamp, g_subln, *, tq=512, tk=512):
    assert tq == tk and T % tq == 0
    nq = T // tq
    return pl.pallas_call(
        functools.partial(_diff_prompt_kernel, tq=tq, tk=tk),
        grid=(B, ATT_HEADS, nq),
        in_specs=[pl.BlockSpec(memory_space=pltpu.SMEM),
                  pl.BlockSpec((tq, ATT_V), lambda b, h, i: (b * nq + i, h)),
                  pl.BlockSpec((T, ATT_V), lambda b, h, i: (b, h)),
                  pl.BlockSpec((T, ATT_V), lambda b, h, i: (b, h)),
                  pl.BlockSpec((4, ATT_QK), lambda b, h, i: (0, 0)),
                  pl.BlockSpec((1, ATT_V), lambda b, h, i: (0, 0))],
        out_specs=pl.BlockSpec((tq, ATT_V), lambda b, h, i: (b * nq + i, h)),
        out_shape=jax.ShapeDtypeStruct((B * T, D_ATT), BF16),
        compiler_params=_cparams(("parallel", "parallel", "arbitrary")),
        name="diff_attn_prompt",
    )(slopes, q, k, v, lamp, g_subln.reshape(1, ATT_V))


PAGES_PER_STEP = 4
QROWS = 16


PAGE_ROWS = PAGE_SIZE * ATT_HEADS
DOT_TN = (((0,), (0,)), ((), ()))


def _diff_sample_kernel(pt_ref, slopes_ref, q_ref, kn_ref, vn_ref, lamp_ref, g_ref, *rest, ds, past):
    k_refs = rest[:PAGES_PER_STEP]
    v_refs = rest[PAGES_PER_STEP:2 * PAGES_PER_STEP]
    o_ref = rest[2 * PAGES_PER_STEP]
    qw_ref, bias_ref, m_ref, l_ref, a_ref = rest[2 * PAGES_PER_STEP + 1:]
    j = pl.program_id(1)
    cph = 2 * ds
    col = lax.broadcasted_iota(jnp.int32, (1, LANES), 1)
    col_head = col // cph
    col_q = col % ds
    slope_col = jnp.zeros((1, LANES), F32)
    for h in range(ATT_HEADS):
        slope_col = jnp.where(col_head == h, slopes_ref[h], slope_col)
    row_key = lax.broadcasted_iota(jnp.int32, (PAGE_ROWS, 1), 0) // ATT_HEADS
    row_head = lax.broadcasted_iota(jnp.int32, (PAGE_ROWS, 1), 0) % ATT_HEADS

    @pl.when(j == 0)
    def _():
        lane = lax.broadcasted_iota(jnp.int32, (ds, ATT_V), 1)
        rows = []
        for h in range(ATT_HEADS):
            qh = q_ref[0, :, h * ATT_V:(h + 1) * ATT_V]
            rows += [jnp.where(lane < ATT_QK, qh, 0.0), jnp.where(lane >= ATT_QK, qh, 0.0)]
        rows.append(jnp.zeros((LANES - ATT_HEADS * cph, ATT_V), F32))
        qw_ref[...] = jnp.concatenate(rows, axis=0).T.astype(BF16)
        bias_ref[...] = jnp.where(row_head == col_head, slope_col * row_key.astype(F32), NEG_INF)
        m_ref[...] = jnp.full(m_ref.shape, NEG_INF, F32)
        l_ref[...] = jnp.zeros(l_ref.shape, F32)
        a_ref[...] = jnp.zeros(a_ref.shape, F32)

    eye = (lax.broadcasted_iota(jnp.int32, (LANES, LANES), 0) == lax.broadcasted_iota(jnp.int32, (LANES, LANES), 1))
    to_col = lambda r: jnp.sum(jnp.where(eye, r, 0.0), axis=1, keepdims=True)

    def attend(k_ref, v_ref, key0, extra):
        k2 = k_ref[0].reshape(PAGE_ROWS, ATT_V).astype(BF16)
        v2 = v_ref[0].reshape(PAGE_ROWS, ATT_V).astype(BF16)
        s = jnp.dot(k2, qw_ref[...], preferred_element_type=F32) * ATT_SCALE + bias_ref[...]
        s = s + slope_col * (key0 - past - col_q).astype(F32)
        if extra is not None:
            s = jnp.where(extra, s, NEG_INF)
        m_old = m_ref[...]
        m_new = jnp.maximum(m_old, jnp.max(s, axis=0, keepdims=True))
        alpha = jnp.exp(m_old - m_new)
        p = jnp.exp(s - m_new)
        l_ref[...] = alpha * l_ref[...] + jnp.sum(p, axis=0, keepdims=True)
        m_ref[...] = m_new
        pv = lax.dot_general(p.astype(BF16), v2, DOT_TN, preferred_element_type=F32)
        a_ref[...] = a_ref[...] * to_col(alpha) + pv

    for i in range(PAGES_PER_STEP):
        attend(k_refs[i], v_refs[i], (j * PAGES_PER_STEP + i) * PAGE_SIZE, None)

    @pl.when(j == pl.num_programs(1) - 1)
    def _():
        attend(kn_ref, vn_ref, past, (row_key < ds) & (row_key <= col_q))
        lam = _lam_from(lamp_ref)
        o_all = a_ref[...] / to_col(l_ref[...])
        for h in range(ATT_HEADS):
            o = o_all[h * cph:h * cph + ds] - lam * o_all[h * cph + ds:(h + 1) * cph]
            o_ref[0, :, h * ATT_V:(h + 1) * ATT_V] = _subln(o, g_ref[...]).astype(o_ref.dtype)


def diff_attn_sample(q, k_new, v_new, cache_k, cache_v, page_table, slopes, lamp, g_subln):
    B, ds, _ = q.shape
    n_pages = page_table.shape[1]
    past = n_pages * PAGE_SIZE
    assert n_pages % PAGES_PER_STEP == 0 and ATT_HEADS * 2 * ds <= LANES
    pad = lambda t: jnp.pad(t, ((0, 0), (0, PAGE_SIZE - ds), (0, 0), (0, 0)))
    page_blk = (1, PAGE_SIZE, ATT_HEADS, ATT_V)

    def page_spec(i):
        return pl.BlockSpec(page_blk, lambda b, j, pt: (pt[b, j * PAGES_PER_STEP + i], 0, 0, 0))

    fixed = lambda shape: pl.BlockSpec(shape, lambda b, j, pt: (0,) * len(shape))
    grid_spec = pltpu.PrefetchScalarGridSpec(
        num_scalar_prefetch=1,
        grid=(B, n_pages // PAGES_PER_STEP),
        in_specs=[pl.BlockSpec(memory_space=pltpu.SMEM),
                  pl.BlockSpec((1, ds, D_ATT), lambda b, j, pt: (b, 0, 0)),
                  pl.BlockSpec(page_blk, lambda b, j, pt: (b, 0, 0, 0)),
                  pl.BlockSpec(page_blk, lambda b, j, pt: (b, 0, 0, 0)),
                  fixed((4, ATT_QK)), fixed((1, ATT_V))]
                 + [page_spec(i) for i in range(PAGES_PER_STEP)] * 2,
        out_specs=pl.BlockSpec((1, ds, D_ATT), lambda b, j, pt: (b, 0, 0)),
        scratch_shapes=[pltpu.VMEM((ATT_V, LANES), BF16),
                        pltpu.VMEM((PAGE_ROWS, LANES), F32),
                        pltpu.VMEM((1, LANES), F32),
                        pltpu.VMEM((1, LANES), F32),
                        pltpu.VMEM((LANES, ATT_V), F32)])
    return pl.pallas_call(
        functools.partial(_diff_sample_kernel, ds=ds, past=past),
        grid_spec=grid_spec,
        out_shape=jax.ShapeDtypeStruct((B, ds, D_ATT), BF16),
        compiler_params=_cparams(("parallel", "arbitrary")),
        name="diff_attn_sample",
    )(page_table, slopes, q, pad(k_new), pad(v_new), lamp, g_subln.reshape(1, ATT_V),
      *([cache_k] * PAGES_PER_STEP), *([cache_v] * PAGES_PER_STEP))


def _mem_attn_kernel(q_ref, k_ref, v_ref, o_ref, *, paged_heads, rows):
    for h in range(MEM_HEADS):
        sl = slice(h * MEM_HEAD, (h + 1) * MEM_HEAD)
        q = q_ref[:, sl] if q_ref.ndim == 2 else q_ref[0, :, sl]
        if q.shape[0] < rows:
            q = jnp.concatenate([q, jnp.zeros((rows - q.shape[0], MEM_HEAD), F32)], axis=0)
        k = k_ref[0, :, h, :] if paged_heads else k_ref[:, sl]
        v = v_ref[0, :, h, :] if paged_heads else v_ref[:, sl]
        s = lax.dot_general(q.astype(BF16), k.astype(BF16), DOT_NT, preferred_element_type=F32) * (MEM_HEAD ** -0.5)
        p = jnp.exp(s - jnp.max(s, axis=-1, keepdims=True))
        o = jnp.dot(p.astype(BF16), v.astype(BF16), preferred_element_type=F32) / jnp.sum(p, axis=-1, keepdims=True)
        if q_ref.ndim == 2:
            o_ref[:, sl] = o.astype(o_ref.dtype)
        else:
            o_ref[0, :, sl] = o[:q_ref.shape[1]].astype(o_ref.dtype)


def mem_attn_prompt(p_c, kv, B, T, MT, *, tq=512):
    tq = min(tq, T)
    nq = T // tq
    return pl.pallas_call(
        functools.partial(_mem_attn_kernel, paged_heads=False, rows=tq),
        grid=(B, nq),
        in_specs=[pl.BlockSpec((tq, D_MEM), lambda b, i: (b * nq + i, 0)),
                  pl.BlockSpec((MT, D_MEM), lambda b, i: (b, 0)),
                  pl.BlockSpec((MT, D_MEM), lambda b, i: (b, 1))],
        out_specs=pl.BlockSpec((tq, D_MEM), lambda b, i: (b * nq + i, 0)),
        out_shape=jax.ShapeDtypeStruct((B * T, D_MEM), BF16),
        compiler_params=_cparams(("parallel", "parallel")),
        name="mem_attn_prompt",
    )(p_c, kv, kv)


def mem_attn_sample(qm, mem_k, mem_v):
    B, ds, _ = qm.shape
    MT = mem_k.shape[1]
    kv_spec = pl.BlockSpec((1, MT, MEM_HEADS, MEM_HEAD), lambda b: (b, 0, 0, 0))
    return pl.pallas_call(
        functools.partial(_mem_attn_kernel, paged_heads=True, rows=QROWS),
        grid=(B,),
        in_specs=[pl.BlockSpec((1, ds, D_MEM), lambda b: (b, 0, 0)), kv_spec, kv_spec],
        out_specs=pl.BlockSpec((1, ds, D_MEM), lambda b: (b, 0, 0)),
        out_shape=jax.ShapeDtypeStruct((B, ds, D_MEM), BF16),
        compiler_params=_cparams(("parallel",)),
        name="mem_attn_sample",
    )(qm, mem_k, mem_v)


PRE_ROWS = 128


def _rwkv7_branch(p_a, B, T, shift_prev, s0, W):
    M = B * T
    sp = jnp.pad(shift_prev, ((0, 0), (0, RW_COLS_PAD - RW_COLS)))
    if T % PRE_ROWS == 0:
        first = jnp.arange(M // PRE_ROWS) * PRE_ROWS
        rows = jnp.where((first % T == 0)[:, None], sp[first // T], p_a[jnp.maximum(first - 1, 0)])
        prev = jnp.zeros((M // PRE_ROWS, 8, RW_COLS_PAD), F32).at[:, 0].set(rows).reshape(-1, RW_COLS_PAD)
        boundary = True
    else:
        p3 = p_a.reshape(B, T, RW_COLS_PAD)
        prev = jnp.concatenate([sp[:, None], p3[:, :-1]], axis=1).reshape(M, RW_COLS_PAD)
        boundary = False
    outs = rwkv_pre(p_a, prev, boundary, T, W, tr=PRE_ROWS)
    per_step = [t.reshape(D_RWKV // LANES, B, T, LANES) for t in outs[:5]]
    v, bonus, g = (t.reshape(B, T, D_RWKV) for t in outs[5:])
    o, s_fin = rwkv_scan(*per_step, v, bonus, g, s0, W['ln_x_w'], W['ln_x_b'])
    shift_last = p_a.reshape(B, T, RW_COLS_PAD)[:, -1, :RW_COLS]
    return o.reshape(M, D_RWKV), s_fin, shift_last


def _layer(x, shift_prev, s0, attend, mem_attend, W):
    B, T, _ = x.shape
    M = B * T
    x2 = x.reshape(M, D_MODEL)
    h = rmsnorm_bf16(x2, W['g_pre_mix'])
    p_a = matmul(h, W['w_in'], n_cols=RW_COLS_PAD)
    q = matmul(h, W['w_in_b'], col0=0, n_cols=D_ATT)
    k = matmul(h, W['w_in_b'], col0=D_ATT, n_cols=D_ATT)
    v = matmul(h, W['w_in_b'], col0=2 * D_ATT, n_cols=D_ATT)
    p_c = matmul(h, W['w_in_b'], col0=3 * D_ATT)

    o_rw, s_fin, shift_last = _rwkv7_branch(p_a, B, T, shift_prev, s0, W)
    o_att = attend(q, k, v)
    o_mem = mem_attend(p_c)

    m = gated_merge(o_rw, o_att, o_mem,
                    W['w_br_rwkv'], W['w_br_attn'], W['w_br_mem'], p_c, D_MEM)
    k = k.reshape(B, T, ATT_HEADS, 2 * ATT_QK)
    v = v.reshape(B, T, ATT_HEADS, ATT_V)
    z = matmul(m, W['w_out'])
    x1, h2 = norm_residual(z, W['g_post_mix'], x2, W['g_pre_mlp'])
    u = matmul(h2, W['w_up'], out_dtype=BF16, relu2=True)
    z2 = matmul(u, W['w_down'], tm=2048, tk=2048)
    y = norm_residual(z2, W['g_post_mlp'], x1)
    return y.reshape(B, T, D_MODEL), k, v, s_fin, shift_last


def kernel(x_prompt, x_sample, cache_k, cache_v, page_table, state_wkv, state_shift, cache_mem_k, cache_mem_v, mem_prompt, g_pre_mix, w_in, mu_shift, w0, w_lora_up, a0, a_lora_up, g_lora_up, k_k, k_a, r_k, ln_x_w, ln_x_b, lambda_q1, lambda_k1, lambda_q2, lambda_k2, g_subln, g_mem, w_mem_kv, w_br_rwkv, w_br_attn, w_br_mem, w_out, g_post_mix, g_pre_mlp, w_up, w_down, g_post_mlp):
    W = dict(g_pre_mix=g_pre_mix, w_in=w_in, w_in_b=w_in[:, RW_COLS:].astype(BF16),
             mu_shift=mu_shift, w0=w0, w_lora_up=w_lora_up,
             a0=a0, a_lora_up=a_lora_up, g_lora_up=g_lora_up, k_k=k_k, k_a=k_a, r_k=r_k,
             ln_x_w=ln_x_w, ln_x_b=ln_x_b, lambda_q1=lambda_q1, lambda_k1=lambda_k1,
             lambda_q2=lambda_q2, lambda_k2=lambda_k2, g_subln=g_subln, w_br_rwkv=w_br_rwkv.astype(BF16),
             w_br_attn=w_br_attn.astype(BF16), w_br_mem=w_br_mem.astype(BF16), w_out=w_out,
             g_post_mix=g_post_mix,
             g_pre_mlp=g_pre_mlp, w_up=w_up, w_down=w_down, g_post_mlp=g_post_mlp)
    slopes = jnp.asarray([2.0 ** (-8.0 * (h + 1) / ATT_HEADS) for h in range(ATT_HEADS)], dtype=F32)
    lamp = jnp.stack([lambda_q1, lambda_k1, lambda_q2, lambda_k2])
    B, T, _ = x_prompt.shape
    BS, DS, _ = x_sample.shape
    MT = mem_prompt.shape[1]

    hm = rmsnorm_bf16(mem_prompt.reshape(B * MT, D_MODEL), g_mem)
    kv = matmul(hm, w_mem_kv)
    mem_k_prompt = kv[:, :D_MEM].reshape(B, MT, MEM_HEADS, MEM_HEAD)
    mem_v_prompt = kv[:, D_MEM:].reshape(B, MT, MEM_HEADS, MEM_HEAD)

    y_prompt, k_prompt, v_prompt, wkv_prompt, shift_prompt = _layer(
        x_prompt, jnp.zeros((B, RW_COLS), F32), jnp.zeros((B, RW_HEADS, RW_HEAD, RW_HEAD), F32),
        lambda q, k, v: diff_attn_prompt(q, k, v, B, T, slopes, lamp, g_subln),
        lambda p_c: mem_attn_prompt(p_c, kv, B, T, MT), W)

    def attend_sample(q, k, v):
        page = lambda t: t.reshape(BS, DS, ATT_HEADS, ATT_V)
        o = diff_attn_sample(q.reshape(BS, DS, D_ATT), page(k), page(v), cache_k, cache_v, page_table,
                             slopes, lamp, g_subln)
        return o.reshape(BS * DS, D_ATT)

    def mem_attend_sample(p_c):
        qm = p_c[:, :D_MEM].reshape(BS, DS, D_MEM)
        return mem_attn_sample(qm, cache_mem_k, cache_mem_v).reshape(BS * DS, D_MEM)

    y_sample, k_sample, v_sample, wkv_sample, shift_sample = _layer(
        x_sample, state_shift, state_wkv, attend_sample, mem_attend_sample, W)
    return (y_prompt, y_sample, k_prompt, v_prompt, k_sample, v_sample,
            wkv_prompt, wkv_sample, shift_prompt, shift_sample, mem_k_prompt, mem_v_prompt)
```

```python
import functools
import math

import jax
import jax.numpy as jnp
from jax import lax
from jax.experimental import pallas as pl
from jax.experimental.pallas import tpu as pltpu

F32 = jnp.float32
BF16 = jnp.bfloat16

D_MODEL = 4096
PAGE_SIZE = 128
D_RWKV = 2048
RW_HEAD = 64
RW_HEADS = 32
DECAY_LORA = 96
AAA_LORA = 96
GATE_LORA = 256
RW_COLS = 3 * D_RWKV + DECAY_LORA + AAA_LORA + GATE_LORA
RW_COLS_PAD = 6656
GN_EPS = 64e-5
D_ATT = 2048
ATT_HEADS = 8
ATT_QK = 128
ATT_V = 256
Q_BLOCK = 128
MEM_HEADS = 4
MEM_HEAD = 128
D_MEM = 512
D_FF = 4 * D_MODEL
NORM_EPS = 1e-6
NEG_INF = -1e30
LAM_INIT = 0.8 - 0.6 * math.exp(-0.3 * 0)

VMEM_LIMIT = 56 * 1024 * 1024
LANES = 128


def _cparams(sem):
    return pltpu.CompilerParams(dimension_semantics=sem, vmem_limit_bytes=VMEM_LIMIT)


def _rmsnorm_kernel(x_ref, g_ref, o_ref):
    x = x_ref[...]
    ms = jnp.mean(x * x, axis=-1, keepdims=True)
    o_ref[...] = (x * lax.rsqrt(ms + NORM_EPS) * g_ref[...]).astype(o_ref.dtype)


def rmsnorm_bf16(x, g, tr=256):
    m, d = x.shape
    tr = min(tr, m)
    return pl.pallas_call(
        _rmsnorm_kernel,
        grid=(m // tr,),
        in_specs=[pl.BlockSpec((tr, d), lambda i: (i, 0)),
                  pl.BlockSpec((1, d), lambda i: (0, 0))],
        out_specs=pl.BlockSpec((tr, d), lambda i: (i, 0)),
        out_shape=jax.ShapeDtypeStruct((m, d), BF16),
        compiler_params=_cparams(("parallel",)),
        name="rmsnorm_bf16",
    )(x, g.reshape(1, d))


DOT_NT = (((1,), (1,)), ((), ()))


def _mm_nt_kernel(a_ref, bt_ref, o_ref):
    o_ref[...] = lax.dot_general(a_ref[...], bt_ref[...].astype(BF16), DOT_NT,
                                 preferred_element_type=F32).astype(o_ref.dtype)


def matmul_nt(a, bt, *, row0=0, n_cols, tm=1024, tn=512):
    m, kd = a.shape
    tm = min(tm, m)
    assert m % tm == 0 and n_cols % tn == 0 and row0 % 8 == 0 and bt.shape[1] == kd
    return pl.pallas_call(
        _mm_nt_kernel,
        grid=(m // tm, n_cols // tn),
        in_specs=[pl.BlockSpec((tm, kd), lambda i, j: (i, 0)),
                  pl.BlockSpec((pl.Element(tn), pl.Element(kd)),
                               lambda i, j: ((row0 // 8 + j * (tn // 8)) * 8, 0))],
        out_specs=pl.BlockSpec((tm, tn), lambda i, j: (i, j)),
        out_shape=jax.ShapeDtypeStruct((m, n_cols), F32),
        compiler_params=_cparams(("parallel", "parallel")),
        name="matmul_nt",
    )(a, bt)


def _mm_kernel(a_ref, b_ref, o_ref, *scratch, relu2, nk):
    def finish(r):
        if relu2:
            r = jnp.square(jnp.maximum(r, 0.0))
        o_ref[...] = r.astype(o_ref.dtype)

    part = jnp.dot(a_ref[...], b_ref[...].astype(BF16), preferred_element_type=F32)
    if nk == 1:
        finish(part)
        return
    acc_ref, = scratch
    k = pl.program_id(2)

    @pl.when(k == 0)
    def _():
        acc_ref[...] = part

    @pl.when(k > 0)
    def _():
        acc_ref[...] += part

    @pl.when(k == nk - 1)
    def _():
        finish(acc_ref[...])


def matmul(a, b, *, tm=1024, tn=512, tk=4096, out_dtype=F32, relu2=False):
    m, kd = a.shape
    n = b.shape[1]
    tm = min(tm, m)
    tk = min(tk, kd)
    assert m % tm == 0 and n % tn == 0 and kd % tk == 0
    nk = kd // tk
    return pl.pallas_call(
        functools.partial(_mm_kernel, relu2=relu2, nk=nk),
        grid=(m // tm, n // tn, nk),
        in_specs=[pl.BlockSpec((tm, tk), lambda i, j, k: (i, k)),
                  pl.BlockSpec((tk, tn), lambda i, j, k: (k, j))],
        out_specs=pl.BlockSpec((tm, tn), lambda i, j, k: (i, j)),
        out_shape=jax.ShapeDtypeStruct((m, n), out_dtype),
        scratch_shapes=[pltpu.VMEM((tm, tn), F32)] if nk > 1 else [],
        compiler_params=_cparams(("parallel", "parallel", "arbitrary")),
        name="matmul",
    )(a, b)


def _norm_res_kernel(z_ref, g_ref, x_ref, *rest, with_next):
    z = z_ref[...]
    ms = jnp.mean(z * z, axis=-1, keepdims=True)
    y = x_ref[...] + z * lax.rsqrt(ms + NORM_EPS) * g_ref[...]
    if with_next:
        g2_ref, o_ref, h_ref = rest
        ms2 = jnp.mean(y * y, axis=-1, keepdims=True)
        h_ref[...] = (y * lax.rsqrt(ms2 + NORM_EPS) * g2_ref[...]).astype(h_ref.dtype)
    else:
        o_ref, = rest
    o_ref[...] = y


def norm_residual(z, g, x, g_next=None, tr=256):
    m, d = z.shape
    tr = min(tr, m)
    row = pl.BlockSpec((tr, d), lambda i: (i, 0))
    vec = pl.BlockSpec((1, d), lambda i: (0, 0))
    with_next = g_next is not None
    args = (z, g.reshape(1, d), x) + ((g_next.reshape(1, d),) if with_next else ())
    return pl.pallas_call(
        functools.partial(_norm_res_kernel, with_next=with_next),
        grid=(m // tr,),
        in_specs=[row, vec, row] + ([vec] if with_next else []),
        out_specs=[row, row] if with_next else row,
        out_shape=([jax.ShapeDtypeStruct((m, d), F32), jax.ShapeDtypeStruct((m, d), BF16)] if with_next
                   else jax.ShapeDtypeStruct((m, d), F32)),
        compiler_params=_cparams(("parallel",)),
        name="norm_residual",
    )(*args)


def _merge_kernel(orw_ref, oatt_ref, omem_ref, wr_ref, wa_ref, wm_ref, grw_ref, gatt_ref, gmem_ref, o_ref):
    def branch(o_ref_, w_ref_, g_ref_):
        y = jnp.dot(o_ref_[...], w_ref_[...], preferred_element_type=F32)
        return jax.nn.sigmoid(g_ref_[...]) * y

    m = branch(orw_ref, wr_ref, grw_ref) + branch(oatt_ref, wa_ref, gatt_ref) + branch(omem_ref, wm_ref, gmem_ref)
    o_ref[...] = m.astype(o_ref.dtype)


def gated_merge(o_rw, o_att, o_mem, w_r, w_a, w_m, p_b, gate_col0, *, tm=1024, tn=512):
    m = o_rw.shape[0]
    tm = min(tm, m)
    gb = gate_col0 // tn
    nb = D_MODEL // tn
    assert gate_col0 % tn == 0

    def gate_spec(b):
        return pl.BlockSpec((tm, tn), lambda i, j: (i, gb + b * nb + j))

    return pl.pallas_call(
        _merge_kernel,
        grid=(m // tm, D_MODEL // tn),
        in_specs=[pl.BlockSpec((tm, D_RWKV), lambda i, j: (i, 0)),
                  pl.BlockSpec((tm, D_ATT), lambda i, j: (i, 0)),
                  pl.BlockSpec((tm, D_MEM), lambda i, j: (i, 0)),
                  pl.BlockSpec((D_RWKV, tn), lambda i, j: (0, j)),
                  pl.BlockSpec((D_ATT, tn), lambda i, j: (0, j)),
                  pl.BlockSpec((D_MEM, tn), lambda i, j: (0, j)),
                  gate_spec(0), gate_spec(1), gate_spec(2)],
        out_specs=pl.BlockSpec((tm, tn), lambda i, j: (i, j)),
        out_shape=jax.ShapeDtypeStruct((m, D_MODEL), BF16),
        compiler_params=_cparams(("parallel", "parallel")),
        name="gated_merge",
    )(o_rw, o_att, o_mem, w_r, w_a, w_m, p_b, p_b, p_b)


RW_QUAD = 4 * RW_HEAD
RW_NQ = D_RWKV // RW_QUAD
RW_ROWS = RW_NQ * RW_HEAD
Y_TILE = 64


LORA_COLS = RW_COLS_PAD - 3 * D_RWKV


def _rwkv_pre_kernel(p_ref, prev_ref, mu_ref, w0_ref, a0_ref, kk_scale_ref, ka_ref, rk_ref,
                     wl_ref, al_ref, gl_ref, ones2_ref,
                     w_o, r_o, k_o, kk_o, b_o, v_o, bonus_o, g_o, *, boundary):
    p = p_ref[...]
    rows = p.shape[0]
    if boundary:
        row_id = lax.broadcasted_iota(jnp.int32, (rows, 1), 0)
        prev = jnp.where(row_id == 0, prev_ref[0:1, :], pltpu.roll(p, 1, 0))
    else:
        prev = prev_ref[...]
    xs = p + (prev - p) * mu_ref[...]
    r = xs[:, 0:D_RWKV]
    k = xs[:, D_RWKV:2 * D_RWKV]
    v = xs[:, 2 * D_RWKV:3 * D_RWKV]
    lo = xs[:, 3 * D_RWKV:]
    mm = lambda x, w_ref_: jnp.dot(x.astype(BF16), w_ref_[...], preferred_element_type=F32)
    w = -jax.nn.softplus(-(w0_ref[...] + mm(jnp.tanh(lo), wl_ref))) - 0.5
    a = jax.nn.sigmoid(a0_ref[...] + mm(lo, al_ref))
    def put(o_ref, q, x):
        o_ref[2 * q] = x[:, :LANES]
        o_ref[2 * q + 1] = x[:, LANES:]

    g_o[...] = mm(jax.nn.sigmoid(lo), gl_ref)
    decay = jnp.exp(-jnp.exp(w))
    k_mod = k * (1.0 + (a - 1.0) * ka_ref[...])
    kk = k * kk_scale_ref[...]
    rk = r * k_mod * rk_ref[...]
    v_o[...] = v
    for q in range(RW_NQ):
        sl = slice(q * RW_QUAD, (q + 1) * RW_QUAD)
        kq = kk[:, sl]
        norm = jnp.maximum(jnp.sqrt(_seg_sum(kq * kq, ones2_ref[...])), 1e-12)
        kq = kq / norm
        put(kk_o, q, kq)
        put(b_o, q, kq * a[:, sl])
        vq = v[:, sl]
        bonus_o[:, sl] = _seg_sum(rk[:, sl], ones2_ref[...]) * vq
        put(w_o, q, decay[:, sl])
        put(r_o, q, r[:, sl])
        put(k_o, q, k_mod[:, sl])


def rwkv_pre(p_a, prev, boundary, T, W, *, tr=128):
    m = p_a.shape[0]
    tr = min(tr, m)
    assert (not boundary or T % tr == 0) and m % tr == 0
    padc = lambda t: jnp.pad(t, (0, RW_COLS_PAD - RW_COLS)).reshape(1, RW_COLS_PAD)
    vec = lambda t: t.reshape(1, D_RWKV)

    def lora_pad(w_up, row0):
        return jnp.zeros((LORA_COLS, D_RWKV), BF16).at[row0:row0 + w_up.shape[0]].set(w_up.astype(BF16))

    ones = _block_diag_ones(BF16)
    ones2 = jnp.concatenate([ones, ones], axis=0)
    row = pl.BlockSpec((tr, RW_COLS_PAD), lambda i: (i, 0))
    prev_spec = pl.BlockSpec((8, RW_COLS_PAD), lambda i: (i, 0)) if boundary else row
    const = lambda shape: pl.BlockSpec(shape, lambda i: (0,) * len(shape))
    out = pl.BlockSpec((tr, D_RWKV), lambda i: (i, 0))
    nlb = D_RWKV // LANES
    out_lb = pl.BlockSpec((nlb, tr, LANES), lambda i: (0, i, 0))
    return pl.pallas_call(
        functools.partial(_rwkv_pre_kernel, boundary=boundary),
        grid=(m // tr,),
        in_specs=[row, prev_spec, const((1, RW_COLS_PAD))] + [const((1, D_RWKV))] * 5
                 + [const((LORA_COLS, D_RWKV))] * 3 + [const((2 * RW_QUAD, RW_QUAD))],
        out_specs=[out_lb] * 5 + [out] * 3,
        out_shape=[jax.ShapeDtypeStruct((nlb, m, LANES), F32)] * 5 + [jax.ShapeDtypeStruct((m, D_RWKV), F32)] * 3,
        compiler_params=_cparams(("parallel",)),
        name="rwkv_pre",
    )(p_a, prev, padc(W['mu_shift']), vec(W['w0']), vec(W['a0']), vec(W['k_k']), vec(W['k_a']), vec(W['r_k']),
      lora_pad(W['w_lora_up'], 0), lora_pad(W['a_lora_up'], DECAY_LORA),
      lora_pad(W['g_lora_up'], DECAY_LORA + AAA_LORA), ones2)


SCAN_BATCH = 4


def _block_diag_ones(dtype):
    r2 = lax.broadcasted_iota(jnp.int32, (RW_QUAD, RW_QUAD), 0) // RW_HEAD
    c2 = lax.broadcasted_iota(jnp.int32, (RW_QUAD, RW_QUAD), 1) // RW_HEAD
    return (r2 == c2).astype(dtype)


def _seg_sum(x, ones2):
    hi = x.astype(BF16)
    lo = (x - hi.astype(F32)).astype(BF16)
    return jnp.dot(jnp.concatenate([hi, lo], axis=1), ones2, preferred_element_type=F32)


def _repeat_row(ref, lb, i, t):
    return ref[lb, i, pl.ds(t, RW_HEAD, stride=0), :]


def _seg_sum_xlane(x):
    first = lax.broadcasted_iota(jnp.int32, (1, LANES), 1) < RW_HEAD
    cols = []
    for c in range(x.shape[1] // LANES):
        xc = x[:, c * LANES:(c + 1) * LANES]
        sa = jnp.sum(jnp.where(first, xc, 0.0), axis=-1, keepdims=True)
        sb = jnp.sum(jnp.where(first, 0.0, xc), axis=-1, keepdims=True)
        cols.append(jnp.where(first, sa, sb))
    return jnp.concatenate(cols, axis=1)


V_HALF = 32


def _scan_kernel(w_ref, r_ref, k_ref, kk_ref, b_ref, v_ref, bonus_ref, g_ref, s0_ref,
                 tsel_ref, ones_ref, ones2_ref, lnw_ref, lnb_ref,
                 o_ref, s_ref, y_ref, nat_ref, vt_ref, vtmp_ref, *, tc, nb):
    c = pl.program_id(1)
    n_half = (tc + V_HALF - 1) // V_HALF

    @pl.when(c == 0)
    def _():
        s_ref[...] = s0_ref[...]

    y_ref[...] = jnp.zeros(y_ref.shape, F32)
    lane_t = lax.broadcasted_iota(jnp.int32, (1, RW_QUAD), 1) % RW_HEAD

    for i in range(nb):
        v = v_ref[i]
        if tc < LANES:
            v = jnp.concatenate([v, jnp.zeros((LANES - tc, D_RWKV), F32)], axis=0)
        vt = v.T
        hi = vt.astype(BF16).astype(F32)
        pieces = (hi, vt - hi)
        for a in range(n_half):
            for q in range(RW_NQ):
                for pc in range(2):
                    for h4 in range(4):
                        src = pieces[pc][(q * 4 + h4) * RW_HEAD:(q * 4 + h4 + 1) * RW_HEAD, a * V_HALF:(a + 1) * V_HALF]
                        dst = (pc * 4 + h4) * V_HALF
                        vtmp_ref[q * RW_HEAD:(q + 1) * RW_HEAD, dst:dst + V_HALF] = src
            vt_ref[a, i * RW_ROWS:(i + 1) * RW_ROWS, :] = vtmp_ref[...].astype(BF16)

    def stack(ref, i, t):
        rep = lambda lb: _repeat_row(ref, lb, i, t)
        return jnp.concatenate(
            [jnp.concatenate([rep(2 * q), rep(2 * q + 1)], axis=1) for q in range(RW_NQ)], axis=0)

    def body(t, carry, a):
        sel = jnp.where(tsel_ref[...] == t - a * V_HALF, 1.0, 0.0).astype(BF16)
        rows = lambda ref: jnp.concatenate([stack(ref, i, t) for i in range(nb)], axis=0)
        s = jnp.concatenate([s_ref[i] for i in range(nb)], axis=0)
        p = s * rows(kk_ref)
        xl = nb * RW_ROWS // 2
        s_kk = jnp.concatenate([_seg_sum_xlane(p[:xl]), _seg_sum(p[xl:], ones2_ref[...])], axis=0)
        v_col = jnp.dot(vt_ref[a], sel, preferred_element_type=F32)
        s = s * rows(w_ref) - s_kk * rows(b_ref) + v_col * rows(k_ref)
        for i in range(nb):
            s_ref[i] = s[i * RW_ROWS:(i + 1) * RW_ROWS]
        y_b = jnp.dot((s * rows(r_ref)).astype(BF16), ones_ref[...], preferred_element_type=F32)
        y_ref[...] = jnp.where(lane_t == t, y_b, y_ref[...])
        return carry

    for a in range(n_half):
        lax.fori_loop(a * V_HALF, min(tc, (a + 1) * V_HALF), functools.partial(body, a=a), 0)

    for i in range(nb):
        for q in range(RW_NQ):
            rq = slice(i * RW_ROWS + q * RW_HEAD, i * RW_ROWS + (q + 1) * RW_HEAD)
            yq = y_ref[rq, :]
            mu = jnp.mean(yq, axis=0, keepdims=True)
            d = yq - mu
            var = jnp.mean(d * d, axis=0, keepdims=True)
            y_ref[rq, :] = d * lax.rsqrt(var + GN_EPS)
        yt = y_ref[i * RW_ROWS:(i + 1) * RW_ROWS, :].T
        for q in range(RW_NQ):
            for h4 in range(4):
                ch = (q * 4 + h4) * RW_HEAD
                nat_ref[:, ch:ch + RW_HEAD] = yt[h4 * RW_HEAD:h4 * RW_HEAD + tc, q * RW_HEAD:(q + 1) * RW_HEAD]
        o = (nat_ref[...] * lnw_ref[...] + lnb_ref[...] + bonus_ref[i]) * g_ref[i]
        o_ref[i] = o.astype(o_ref.dtype)


def rwkv_scan(w, r, k, kk, b, v, bonus, g, s0, ln_w, ln_b):
    B, T, _ = bonus.shape
    tc = min(Y_TILE, T)
    nc = T // tc
    nb = SCAN_BATCH
    assert T % tc == 0 and B % nb == 0
    s0q = s0.reshape(B, RW_NQ, 4, RW_HEAD, RW_HEAD).transpose(0, 1, 3, 2, 4).reshape(B, RW_ROWS, RW_QUAD)
    rr = lax.broadcasted_iota(jnp.int32, (RW_QUAD, RW_QUAD), 0)
    cc = lax.broadcasted_iota(jnp.int32, (RW_QUAD, RW_QUAD), 1)
    tsel = jnp.where((rr // V_HALF) % 4 == cc // RW_HEAD, rr % V_HALF, -1)
    ones = _block_diag_ones(BF16)
    ones2 = jnp.concatenate([ones, ones], axis=0)
    xspec = pl.BlockSpec((nb, tc, D_RWKV), lambda bi, ci: (bi, ci, 0))
    sspec = pl.BlockSpec((nb, RW_ROWS, RW_QUAD), lambda bi, ci: (bi, 0, 0))
    const = lambda shape: pl.BlockSpec(shape, lambda bi, ci: (0,) * len(shape))
    o, s_fin = pl.pallas_call(
        functools.partial(_scan_kernel, tc=tc, nb=nb),
        grid=(B // nb, nc),
        in_specs=[pl.BlockSpec((D_RWKV // LANES, nb, tc, LANES), lambda bi, ci: (0, bi, ci, 0))] * 5
                 + [xspec] * 3 + [sspec, const((RW_QUAD, RW_QUAD)),
                                const((RW_QUAD, RW_QUAD)), const((2 * RW_QUAD, RW_QUAD)),
                                const((1, D_RWKV)), const((1, D_RWKV))],
        out_specs=[xspec, sspec],
        out_shape=[jax.ShapeDtypeStruct((B, T, D_RWKV), BF16),
                   jax.ShapeDtypeStruct((B, RW_ROWS, RW_QUAD), F32)],
        scratch_shapes=[pltpu.VMEM((nb * RW_ROWS, RW_QUAD), F32), pltpu.VMEM((tc, D_RWKV), F32),
                        pltpu.VMEM(((tc + V_HALF - 1) // V_HALF, nb * RW_ROWS, RW_QUAD), BF16),
                        pltpu.VMEM((RW_ROWS, RW_QUAD), F32)],
        compiler_params=_cparams(("parallel", "arbitrary")),
        name="rwkv_scan",
    )(w, r, k, kk, b, v, bonus, g, s0q, tsel, ones, ones2,
      ln_w.reshape(1, D_RWKV), ln_b.reshape(1, D_RWKV))
    s_fin = s_fin.reshape(B, RW_NQ, RW_HEAD, 4, RW_HEAD).transpose(0, 1, 3, 2, 4).reshape(B, RW_HEADS, RW_HEAD, RW_HEAD)
    return o, s_fin


ATT_SCALE = ATT_QK ** -0.5


def _lam_from(lamp_ref):
    lp = lamp_ref[...]
    return (jnp.exp(jnp.sum(lp[0:1] * lp[1:2])) - jnp.exp(jnp.sum(lp[2:3] * lp[3:4])) + LAM_INIT)


def _online_softmax_step(s, v_bf, m, l, a):
    m_new = jnp.maximum(m, jnp.max(s, axis=-1, keepdims=True))
    alpha = jnp.exp(m - m_new)
    p = jnp.exp(s - m_new)
    l = alpha * l + jnp.sum(p, axis=-1, keepdims=True)
    a = alpha * a + jnp.dot(p.astype(BF16), v_bf, preferred_element_type=F32)
    return m_new, l, a


def _subln(o, g):
    ms = jnp.mean(o * o, axis=-1, keepdims=True)
    return o * lax.rsqrt(ms + NORM_EPS) * g * (1.0 - LAM_INIT)


def _diff_prompt_kernel(slopes_ref, q_ref, k_ref, v_ref, lamp_ref, g_ref, o_ref, *, tq, tk):
    h = pl.program_id(1)
    qi = pl.program_id(2)
    slope = slopes_ref[h]
    lam = _lam_from(lamp_ref)
    q = q_ref[...]
    q1 = q[:, :ATT_QK].astype(BF16)
    q2 = q[:, ATT_QK:].astype(BF16)
    rel = (lax.broadcasted_iota(jnp.int32, (tq, tk), 0) - lax.broadcasted_iota(jnp.int32, (tq, tk), 1))
    base = -slope * rel.astype(F32)

    def kv_step(kb, carry, diagonal=False):
        m1, l1, a1, m2, l2, a2 = carry
        k0 = pl.multiple_of(kb * tk, tk)
        k = k_ref[pl.ds(k0, tk), :]
        v = v_ref[pl.ds(k0, tk), :].astype(BF16)
        bias = base - slope * (qi * tq - k0).astype(F32)
        if diagonal:
            bias = jnp.where(rel >= 0, bias, NEG_INF)
        s1 = lax.dot_general(q1, k[:, :ATT_QK].astype(BF16), DOT_NT, preferred_element_type=F32) * ATT_SCALE + bias
        s2 = lax.dot_general(q2, k[:, ATT_QK:].astype(BF16), DOT_NT, preferred_element_type=F32) * ATT_SCALE + bias
        m1, l1, a1 = _online_softmax_step(s1, v, m1, l1, a1)
        m2, l2, a2 = _online_softmax_step(s2, v, m2, l2, a2)
        return m1, l1, a1, m2, l2, a2

    col = lambda val: jnp.full((tq, 1), val, F32)
    acc0 = jnp.zeros((tq, ATT_V), F32)
    carry = lax.fori_loop(0, qi, kv_step, (col(NEG_INF), col(0.0), acc0, col(NEG_INF), col(0.0), acc0))
    m1, l1, a1, m2, l2, a2 = kv_step(qi, carry, diagonal=True)
    o = a1 / l1 - lam * (a2 / l2)
    o_ref[...] = _subln(o, g_ref[...]).astype(o_ref.dtype)


def diff_attn_prompt(q, k, v, B, T, slopes, lamp, g_subln, *, tq=512, tk=512):
    assert tq == tk and T % tq == 0
    nq = T // tq
    return pl.pallas_call(
        functools.partial(_diff_prompt_kernel, tq=tq, tk=tk),
        grid=(B, ATT_HEADS, nq),
        in_specs=[pl.BlockSpec(memory_space=pltpu.SMEM),
                  pl.BlockSpec((tq, ATT_V), lambda b, h, i: (b * nq + i, h)),
                  pl.BlockSpec((T, ATT_V), lambda b, h, i: (b, h)),
                  pl.BlockSpec((T, ATT_V), lambda b, h, i: (b, h)),
                  pl.BlockSpec((4, ATT_QK), lambda b, h, i: (0, 0)),
                  pl.BlockSpec((1, ATT_V), lambda b, h, i: (0, 0))],
        out_specs=pl.BlockSpec((tq, ATT_V), lambda b, h, i: (b * nq + i, h)),
        out_shape=jax.ShapeDtypeStruct((B * T, D_ATT), BF16),
        compiler_params=_cparams(("parallel", "parallel", "arbitrary")),
        name="diff_attn_prompt",
    )(slopes, q, k, v, lamp, g_subln.reshape(1, ATT_V))


PAGES_PER_STEP = 4
QROWS = 16


PAGE_ROWS = PAGE_SIZE * ATT_HEADS
DOT_TN = (((0,), (0,)), ((), ()))


def _diff_sample_kernel(pt_ref, slopes_ref, q_ref, kn_ref, vn_ref, lamp_ref, g_ref, *rest, ds, past):
    k_refs = rest[:PAGES_PER_STEP]
    v_refs = rest[PAGES_PER_STEP:2 * PAGES_PER_STEP]
    o_ref = rest[2 * PAGES_PER_STEP]
    qw_ref, bias_ref, m_ref, l_ref, a_ref = rest[2 * PAGES_PER_STEP + 1:]
    j = pl.program_id(1)
    cph = 2 * ds
    col = lax.broadcasted_iota(jnp.int32, (1, LANES), 1)
    col_head = col // cph
    col_q = col % ds
    slope_col = jnp.zeros((1, LANES), F32)
    for h in range(ATT_HEADS):
        slope_col = jnp.where(col_head == h, slopes_ref[h], slope_col)
    row_key = lax.broadcasted_iota(jnp.int32, (PAGE_ROWS, 1), 0) // ATT_HEADS
    row_head = lax.broadcasted_iota(jnp.int32, (PAGE_ROWS, 1), 0) % ATT_HEADS

    @pl.when(j == 0)
    def _():
        lane = lax.broadcasted_iota(jnp.int32, (ds, ATT_V), 1)
        rows = []
        for h in range(ATT_HEADS):
            qh = q_ref[0, :, h * ATT_V:(h + 1) * ATT_V]
            rows += [jnp.where(lane < ATT_QK, qh, 0.0), jnp.where(lane >= ATT_QK, qh, 0.0)]
        rows.append(jnp.zeros((LANES - ATT_HEADS * cph, ATT_V), F32))
        qw_ref[...] = jnp.concatenate(rows, axis=0).T.astype(BF16)
        bias_ref[...] = jnp.where(row_head == col_head, slope_col * row_key.astype(F32), NEG_INF)
        m_ref[...] = jnp.full(m_ref.shape, NEG_INF, F32)
        l_ref[...] = jnp.zeros(l_ref.shape, F32)
        a_ref[...] = jnp.zeros(a_ref.shape, F32)

    eye = (lax.broadcasted_iota(jnp.int32, (LANES, LANES), 0) == lax.broadcasted_iota(jnp.int32, (LANES, LANES), 1))
    to_col = lambda r: jnp.sum(jnp.where(eye, r, 0.0), axis=1, keepdims=True)

    def attend(k_list, v_list, key0, extra):
        s_pages = []
        for n, k_ref in enumerate(k_list):
            k2 = k_ref[0].reshape(PAGE_ROWS, ATT_V).astype(BF16)
            s = jnp.dot(k2, qw_ref[...], preferred_element_type=F32) * ATT_SCALE + bias_ref[...]
            s_pages.append(s + slope_col * (key0 + n * PAGE_SIZE - past - col_q).astype(F32))
        s = jnp.concatenate(s_pages, axis=0)
        if extra is not None:
            s = jnp.where(extra, s, NEG_INF)
        v2 = jnp.concatenate([v_ref[0].reshape(PAGE_ROWS, ATT_V).astype(BF16) for v_ref in v_list], axis=0)
        m_old = m_ref[...]
        m_new = jnp.maximum(m_old, jnp.max(s, axis=0, keepdims=True))
        alpha = jnp.exp(m_old - m_new)
        p = jnp.exp(s - m_new)
        l_ref[...] = alpha * l_ref[...] + jnp.sum(p, axis=0, keepdims=True)
        m_ref[...] = m_new
        pv = lax.dot_general(p.astype(BF16), v2, DOT_TN, preferred_element_type=F32)
        a_ref[...] = a_ref[...] * to_col(alpha) + pv

    attend(k_refs, v_refs, j * PAGES_PER_STEP * PAGE_SIZE, None)

    @pl.when(j == pl.num_programs(1) - 1)
    def _():
        attend([kn_ref], [vn_ref], past, (row_key < ds) & (row_key <= col_q))
        lam = _lam_from(lamp_ref)
        o_all = a_ref[...] / to_col(l_ref[...])
        for h in range(ATT_HEADS):
            o = o_all[h * cph:h * cph + ds] - lam * o_all[h * cph + ds:(h + 1) * cph]
            o_ref[0, :, h * ATT_V:(h + 1) * ATT_V] = _subln(o, g_ref[...]).astype(o_ref.dtype)


def diff_attn_sample(q, k_new, v_new, cache_k, cache_v, page_table, slopes, lamp, g_subln):
    B, ds, _ = q.shape
    n_pages = page_table.shape[1]
    past = n_pages * PAGE_SIZE
    assert n_pages % PAGES_PER_STEP == 0 and ATT_HEADS * 2 * ds <= LANES
    pad = lambda t: jnp.pad(t, ((0, 0), (0, PAGE_SIZE - ds), (0, 0), (0, 0)))
    page_blk = (1, PAGE_SIZE, ATT_HEADS, ATT_V)

    def page_spec(i):
        return pl.BlockSpec(page_blk, lambda b, j, pt: (pt[b, j * PAGES_PER_STEP + i], 0, 0, 0))

    fixed = lambda shape: pl.BlockSpec(shape, lambda b, j, pt: (0,) * len(shape))
    grid_spec = pltpu.PrefetchScalarGridSpec(
        num_scalar_prefetch=1,
        grid=(B, n_pages // PAGES_PER_STEP),
        in_specs=[pl.BlockSpec(memory_space=pltpu.SMEM),
                  pl.BlockSpec((1, ds, D_ATT), lambda b, j, pt: (b, 0, 0)),
                  pl.BlockSpec(page_blk, lambda b, j, pt: (b, 0, 0, 0)),
                  pl.BlockSpec(page_blk, lambda b, j, pt: (b, 0, 0, 0)),
                  fixed((4, ATT_QK)), fixed((1, ATT_V))]
                 + [page_spec(i) for i in range(PAGES_PER_STEP)] * 2,
        out_specs=pl.BlockSpec((1, ds, D_ATT), lambda b, j, pt: (b, 0, 0)),
        scratch_shapes=[pltpu.VMEM((ATT_V, LANES), BF16),
                        pltpu.VMEM((PAGE_ROWS, LANES), F32),
                        pltpu.VMEM((1, LANES), F32),
                        pltpu.VMEM((1, LANES), F32),
                        pltpu.VMEM((LANES, ATT_V), F32)])
    return pl.pallas_call(
        functools.partial(_diff_sample_kernel, ds=ds, past=past),
        grid_spec=grid_spec,
        out_shape=jax.ShapeDtypeStruct((B, ds, D_ATT), BF16),
        compiler_params=_cparams(("parallel", "arbitrary")),
        name="diff_attn_sample",
    )(page_table, slopes, q, pad(k_new), pad(v_new), lamp, g_subln.reshape(1, ATT_V),
      *([cache_k] * PAGES_PER_STEP), *([cache_v] * PAGES_PER_STEP))


def _mem_attn_kernel(q_ref, k_ref, v_ref, o_ref, *, paged_heads, rows):
    for h in range(MEM_HEADS):
        sl = slice(h * MEM_HEAD, (h + 1) * MEM_HEAD)
        q = q_ref[:, sl] if q_ref.ndim == 2 else q_ref[0, :, sl]
        if q.shape[0] < rows:
            q = jnp.concatenate([q, jnp.zeros((rows - q.shape[0], MEM_HEAD), F32)], axis=0)
        k = k_ref[0, :, h, :] if paged_heads else k_ref[:, sl]
        v = v_ref[0, :, h, :] if paged_heads else v_ref[:, sl]
        s = lax.dot_general(q.astype(BF16), k.astype(BF16), DOT_NT, preferred_element_type=F32) * (MEM_HEAD ** -0.5)
        p = jnp.exp(s - jnp.max(s, axis=-1, keepdims=True))
        o = jnp.dot(p.astype(BF16), v.astype(BF16), preferred_element_type=F32) / jnp.sum(p, axis=-1, keepdims=True)
        if q_ref.ndim == 2:
            o_ref[:, sl] = o.astype(o_ref.dtype)
        else:
            o_ref[0, :, sl] = o[:q_ref.shape[1]].astype(o_ref.dtype)


def mem_attn_prompt(p_c, kv, B, T, MT, *, tq=512):
    tq = min(tq, T)
    nq = T // tq
    return pl.pallas_call(
        functools.partial(_mem_attn_kernel, paged_heads=False, rows=tq),
        grid=(B, nq),
        in_specs=[pl.BlockSpec((tq, D_MEM), lambda b, i: (b * nq + i, 0)),
                  pl.BlockSpec((MT, D_MEM), lambda b, i: (b, 0)),
                  pl.BlockSpec((MT, D_MEM), lambda b, i: (b, 1))],
        out_specs=pl.BlockSpec((tq, D_MEM), lambda b, i: (b * nq + i, 0)),
        out_shape=jax.ShapeDtypeStruct((B * T, D_MEM), BF16),
        compiler_params=_cparams(("parallel", "parallel")),
        name="mem_attn_prompt",
    )(p_c, kv, kv)


def mem_attn_sample(qm, mem_k, mem_v):
    B, ds, _ = qm.shape
    MT = mem_k.shape[1]
    kv_spec = pl.BlockSpec((1, MT, MEM_HEADS, MEM_HEAD), lambda b: (b, 0, 0, 0))
    return pl.pallas_call(
        functools.partial(_mem_attn_kernel, paged_heads=True, rows=QROWS),
        grid=(B,),
        in_specs=[pl.BlockSpec((1, ds, D_MEM), lambda b: (b, 0, 0)), kv_spec, kv_spec],
        out_specs=pl.BlockSpec((1, ds, D_MEM), lambda b: (b, 0, 0)),
        out_shape=jax.ShapeDtypeStruct((B, ds, D_MEM), BF16),
        compiler_params=_cparams(("parallel",)),
        name="mem_attn_sample",
    )(qm, mem_k, mem_v)


PRE_ROWS = 128


def _rwkv7_branch(p_a, B, T, shift_prev, s0, W):
    M = B * T
    sp = jnp.pad(shift_prev, ((0, 0), (0, RW_COLS_PAD - RW_COLS)))
    if T % PRE_ROWS == 0:
        first = jnp.arange(M // PRE_ROWS) * PRE_ROWS
        rows = jnp.where((first % T == 0)[:, None], sp[first // T], p_a[jnp.maximum(first - 1, 0)])
        prev = jnp.zeros((M // PRE_ROWS, 8, RW_COLS_PAD), F32).at[:, 0].set(rows).reshape(-1, RW_COLS_PAD)
        boundary = True
    else:
        p3 = p_a.reshape(B, T, RW_COLS_PAD)
        prev = jnp.concatenate([sp[:, None], p3[:, :-1]], axis=1).reshape(M, RW_COLS_PAD)
        boundary = False
    outs = rwkv_pre(p_a, prev, boundary, T, W, tr=PRE_ROWS)
    per_step = [t.reshape(D_RWKV // LANES, B, T, LANES) for t in outs[:5]]
    v, bonus, g = (t.reshape(B, T, D_RWKV) for t in outs[5:])
    o, s_fin = rwkv_scan(*per_step, v, bonus, g, s0, W['ln_x_w'], W['ln_x_b'])
    shift_last = p_a.reshape(B, T, RW_COLS_PAD)[:, -1, :RW_COLS]
    return o.reshape(M, D_RWKV), s_fin, shift_last


def _layer(x, shift_prev, s0, attend, mem_attend, W):
    B, T, _ = x.shape
    M = B * T
    x2 = x.reshape(M, D_MODEL)
    h = rmsnorm_bf16(x2, W['g_pre_mix'])
    wt = W['w_in_t']
    p_a = matmul_nt(h, wt, n_cols=RW_COLS_PAD)
    q = matmul_nt(h, wt, row0=RW_COLS, n_cols=D_ATT)
    k = matmul_nt(h, wt, row0=RW_COLS + D_ATT, n_cols=D_ATT)
    v = matmul_nt(h, wt, row0=RW_COLS + 2 * D_ATT, n_cols=D_ATT)
    p_c = matmul_nt(h, wt, row0=RW_COLS + 3 * D_ATT, n_cols=D_MEM + 3 * D_MODEL)

    o_rw, s_fin, shift_last = _rwkv7_branch(p_a, B, T, shift_prev, s0, W)
    o_att = attend(q, k, v)
    o_mem = mem_attend(p_c)

    m = gated_merge(o_rw, o_att, o_mem,
                    W['w_br_rwkv'], W['w_br_attn'], W['w_br_mem'], p_c, D_MEM)
    k = k.reshape(B, T, ATT_HEADS, 2 * ATT_QK)
    v = v.reshape(B, T, ATT_HEADS, ATT_V)
    z = matmul(m, W['w_out'])
    x1, h2 = norm_residual(z, W['g_post_mix'], x2, W['g_pre_mlp'])
    u = matmul(h2, W['w_up'], out_dtype=BF16, relu2=True)
    z2 = matmul(u, W['w_down'], tm=2048, tn=1024, tk=1024)
    y = norm_residual(z2, W['g_post_mlp'], x1)
    return y.reshape(B, T, D_MODEL), k, v, s_fin, shift_last


def kernel(x_prompt, x_sample, cache_k, cache_v, page_table, state_wkv, state_shift, cache_mem_k, cache_mem_v, mem_prompt, g_pre_mix, w_in, mu_shift, w0, w_lora_up, a0, a_lora_up, g_lora_up, k_k, k_a, r_k, ln_x_w, ln_x_b, lambda_q1, lambda_k1, lambda_q2, lambda_k2, g_subln, g_mem, w_mem_kv, w_br_rwkv, w_br_attn, w_br_mem, w_out, g_post_mix, g_pre_mlp, w_up, w_down, g_post_mlp):
    W = dict(g_pre_mix=g_pre_mix, w_in_t=w_in.T,
             mu_shift=mu_shift, w0=w0, w_lora_up=w_lora_up,
             a0=a0, a_lora_up=a_lora_up, g_lora_up=g_lora_up, k_k=k_k, k_a=k_a, r_k=r_k,
             ln_x_w=ln_x_w, ln_x_b=ln_x_b, lambda_q1=lambda_q1, lambda_k1=lambda_k1,
             lambda_q2=lambda_q2, lambda_k2=lambda_k2, g_subln=g_subln, w_br_rwkv=w_br_rwkv.astype(BF16),
             w_br_attn=w_br_attn.astype(BF16), w_br_mem=w_br_mem.astype(BF16), w_out=w_out,
             g_post_mix=g_post_mix,
             g_pre_mlp=g_pre_mlp, w_up=w_up, w_down=w_down, g_post_mlp=g_post_mlp)
    slopes = jnp.asarray([2.0 ** (-8.0 * (h + 1) / ATT_HEADS) for h in range(ATT_HEADS)], dtype=F32)
    lamp = jnp.stack([lambda_q1, lambda_k1, lambda_q2, lambda_k2])
    B, T, _ = x_prompt.shape
    BS, DS, _ = x_sample.shape
    MT = mem_prompt.shape[1]

    hm = rmsnorm_bf16(mem_prompt.reshape(B * MT, D_MODEL), g_mem)
    kv = matmul(hm, w_mem_kv)
    mem_k_prompt = kv[:, :D_MEM].reshape(B, MT, MEM_HEADS, MEM_HEAD)
    mem_v_prompt = kv[:, D_MEM:].reshape(B, MT, MEM_HEADS, MEM_HEAD)

    y_prompt, k_prompt, v_prompt, wkv_prompt, shift_prompt = _layer(
        x_prompt, jnp.zeros((B, RW_COLS), F32), jnp.zeros((B, RW_HEADS, RW_HEAD, RW_HEAD), F32),
        lambda q, k, v: diff_attn_prompt(q, k, v, B, T, slopes, lamp, g_subln),
        lambda p_c: mem_attn_prompt(p_c, kv, B, T, MT), W)

    def attend_sample(q, k, v):
        page = lambda t: t.reshape(BS, DS, ATT_HEADS, ATT_V)
        o = diff_attn_sample(q.reshape(BS, DS, D_ATT), page(k), page(v), cache_k, cache_v, page_table,
                             slopes, lamp, g_subln)
        return o.reshape(BS * DS, D_ATT)

    def mem_attend_sample(p_c):
        qm = p_c[:, :D_MEM].reshape(BS, DS, D_MEM)
        return mem_attn_sample(qm, cache_mem_k, cache_mem_v).reshape(BS * DS, D_MEM)

    y_sample, k_sample, v_sample, wkv_sample, shift_sample = _layer(
        x_sample, state_shift, state_wkv, attend_sample, mem_attend_sample, W)
    return (y_prompt, y_sample, k_prompt, v_prompt, k_sample, v_sample,
            wkv_prompt, wkv_sample, shift_prompt, shift_sample, mem_k_prompt, mem_v_prompt)
```

```python
import functools
import math

import jax
import jax.numpy as jnp
from jax import lax
from jax.experimental import pallas as pl
from jax.experimental.pallas import tpu as pltpu

F32 = jnp.float32
BF16 = jnp.bfloat16

D_MODEL = 4096
PAGE_SIZE = 128
D_RWKV = 2048
RW_HEAD = 64
RW_HEADS = 32
DECAY_LORA = 96
AAA_LORA = 96
GATE_LORA = 256
RW_COLS = 3 * D_RWKV + DECAY_LORA + AAA_LORA + GATE_LORA
RW_COLS_PAD = 6656
GN_EPS = 64e-5
D_ATT = 2048
ATT_HEADS = 8
ATT_QK = 128
ATT_V = 256
Q_BLOCK = 128
MEM_HEADS = 4
MEM_HEAD = 128
D_MEM = 512
D_FF = 4 * D_MODEL
NORM_EPS = 1e-6
NEG_INF = -1e30
LAM_INIT = 0.8 - 0.6 * math.exp(-0.3 * 0)

VMEM_LIMIT = 56 * 1024 * 1024
LANES = 128


def _cparams(sem):
    return pltpu.CompilerParams(dimension_semantics=sem, vmem_limit_bytes=VMEM_LIMIT)


def _rmsnorm_kernel(x_ref, g_ref, o_ref):
    x = x_ref[...]
    ms = jnp.mean(x * x, axis=-1, keepdims=True)
    o_ref[...] = (x * lax.rsqrt(ms + NORM_EPS) * g_ref[...]).astype(o_ref.dtype)


def rmsnorm_bf16(x, g, tr=256):
    m, d = x.shape
    tr = min(tr, m)
    return pl.pallas_call(
        _rmsnorm_kernel,
        grid=(m // tr,),
        in_specs=[pl.BlockSpec((tr, d), lambda i: (i, 0)),
                  pl.BlockSpec((1, d), lambda i: (0, 0))],
        out_specs=pl.BlockSpec((tr, d), lambda i: (i, 0)),
        out_shape=jax.ShapeDtypeStruct((m, d), BF16),
        compiler_params=_cparams(("parallel",)),
        name="rmsnorm_bf16",
    )(x, g.reshape(1, d))


DOT_NT = (((1,), (1,)), ((), ()))


def _mm_nt_kernel(a_ref, bt_ref, o_ref):
    o_ref[...] = lax.dot_general(a_ref[...], bt_ref[...].astype(BF16), DOT_NT,
                                 preferred_element_type=F32).astype(o_ref.dtype)


def matmul_nt(a, bt, *, row0=0, n_cols, tm=2048, tn=512):
    m, kd = a.shape
    tm = min(tm, m)
    assert m % tm == 0 and n_cols % tn == 0 and row0 % 8 == 0 and bt.shape[1] == kd
    return pl.pallas_call(
        _mm_nt_kernel,
        grid=(m // tm, n_cols // tn),
        in_specs=[pl.BlockSpec((tm, kd), lambda i, j: (i, 0), pipeline_mode=pl.Buffered(1)),
                  pl.BlockSpec((pl.Element(tn), pl.Element(kd)),
                               lambda i, j: ((row0 // 8 + j * (tn // 8)) * 8, 0))],
        out_specs=pl.BlockSpec((tm, tn), lambda i, j: (i, j)),
        out_shape=jax.ShapeDtypeStruct((m, n_cols), F32),
        compiler_params=_cparams(("parallel", "parallel")),
        name="matmul_nt",
    )(a, bt)


def _mm_kernel(a_ref, b_ref, o_ref, *scratch, relu2, nk):
    def finish(r):
        if relu2:
            r = jnp.square(jnp.maximum(r, 0.0))
        o_ref[...] = r.astype(o_ref.dtype)

    part = jnp.dot(a_ref[...], b_ref[...].astype(BF16), preferred_element_type=F32)
    if nk == 1:
        finish(part)
        return
    acc_ref, = scratch
    k = pl.program_id(2)

    @pl.when(k == 0)
    def _():
        acc_ref[...] = part

    @pl.when(k > 0)
    def _():
        acc_ref[...] += part

    @pl.when(k == nk - 1)
    def _():
        finish(acc_ref[...])


def matmul(a, b, *, tm=2048, tn=512, tk=4096, out_dtype=F32, relu2=False):
    m, kd = a.shape
    n = b.shape[1]
    tm = min(tm, m)
    tk = min(tk, kd)
    assert m % tm == 0 and n % tn == 0 and kd % tk == 0
    nk = kd // tk
    a_mode = dict(pipeline_mode=pl.Buffered(1)) if nk == 1 else {}
    return pl.pallas_call(
        functools.partial(_mm_kernel, relu2=relu2, nk=nk),
        grid=(m // tm, n // tn, nk),
        in_specs=[pl.BlockSpec((tm, tk), lambda i, j, k: (i, k), **a_mode),
                  pl.BlockSpec((tk, tn), lambda i, j, k: (k, j))],
        out_specs=pl.BlockSpec((tm, tn), lambda i, j, k: (i, j)),
        out_shape=jax.ShapeDtypeStruct((m, n), out_dtype),
        scratch_shapes=[pltpu.VMEM((tm, tn), F32)] if nk > 1 else [],
        compiler_params=_cparams(("parallel", "parallel", "arbitrary")),
        name="matmul",
    )(a, b)


def _norm_res_kernel(z_ref, g_ref, x_ref, *rest, with_next):
    z = z_ref[...]
    ms = jnp.mean(z * z, axis=-1, keepdims=True)
    y = x_ref[...] + z * lax.rsqrt(ms + NORM_EPS) * g_ref[...]
    if with_next:
        g2_ref, o_ref, h_ref = rest
        ms2 = jnp.mean(y * y, axis=-1, keepdims=True)
        h_ref[...] = (y * lax.rsqrt(ms2 + NORM_EPS) * g2_ref[...]).astype(h_ref.dtype)
    else:
        o_ref, = rest
    o_ref[...] = y


def norm_residual(z, g, x, g_next=None, tr=256):
    m, d = z.shape
    tr = min(tr, m)
    row = pl.BlockSpec((tr, d), lambda i: (i, 0))
    vec = pl.BlockSpec((1, d), lambda i: (0, 0))
    with_next = g_next is not None
    args = (z, g.reshape(1, d), x) + ((g_next.reshape(1, d),) if with_next else ())
    return pl.pallas_call(
        functools.partial(_norm_res_kernel, with_next=with_next),
        grid=(m // tr,),
        in_specs=[row, vec, row] + ([vec] if with_next else []),
        out_specs=[row, row] if with_next else row,
        out_shape=([jax.ShapeDtypeStruct((m, d), F32), jax.ShapeDtypeStruct((m, d), BF16)] if with_next
                   else jax.ShapeDtypeStruct((m, d), F32)),
        compiler_params=_cparams(("parallel",)),
        name="norm_residual",
    )(*args)


def _merge_kernel(orw_ref, oatt_ref, omem_ref, wr_ref, wa_ref, wm_ref, grw_ref, gatt_ref, gmem_ref, o_ref):
    def branch(o_ref_, w_ref_, g_ref_):
        y = jnp.dot(o_ref_[...], w_ref_[...], preferred_element_type=F32)
        return jax.nn.sigmoid(g_ref_[...]) * y

    m = branch(orw_ref, wr_ref, grw_ref) + branch(oatt_ref, wa_ref, gatt_ref) + branch(omem_ref, wm_ref, gmem_ref)
    o_ref[...] = m.astype(o_ref.dtype)


def gated_merge(o_rw, o_att, o_mem, w_r, w_a, w_m, p_b, gate_col0, *, tm=1024, tn=512):
    m = o_rw.shape[0]
    tm = min(tm, m)
    gb = gate_col0 // tn
    nb = D_MODEL // tn
    assert gate_col0 % tn == 0

    def gate_spec(b):
        return pl.BlockSpec((tm, tn), lambda i, j: (i, gb + b * nb + j))

    return pl.pallas_call(
        _merge_kernel,
        grid=(m // tm, D_MODEL // tn),
        in_specs=[pl.BlockSpec((tm, D_RWKV), lambda i, j: (i, 0)),
                  pl.BlockSpec((tm, D_ATT), lambda i, j: (i, 0)),
                  pl.BlockSpec((tm, D_MEM), lambda i, j: (i, 0)),
                  pl.BlockSpec((D_RWKV, tn), lambda i, j: (0, j)),
                  pl.BlockSpec((D_ATT, tn), lambda i, j: (0, j)),
                  pl.BlockSpec((D_MEM, tn), lambda i, j: (0, j)),
                  gate_spec(0), gate_spec(1), gate_spec(2)],
        out_specs=pl.BlockSpec((tm, tn), lambda i, j: (i, j)),
        out_shape=jax.ShapeDtypeStruct((m, D_MODEL), BF16),
        compiler_params=_cparams(("parallel", "parallel")),
        name="gated_merge",
    )(o_rw, o_att, o_mem, w_r, w_a, w_m, p_b, p_b, p_b)


RW_QUAD = 4 * RW_HEAD
RW_NQ = D_RWKV // RW_QUAD
RW_ROWS = RW_NQ * RW_HEAD
Y_TILE = 64


LORA_COLS = RW_COLS_PAD - 3 * D_RWKV


def _rwkv_pre_kernel(p_ref, prev_ref, mu_ref, w0_ref, a0_ref, kk_scale_ref, ka_ref, rk_ref,
                     wl_ref, al_ref, gl_ref, ones2_ref,
                     w_o, r_o, k_o, kk_o, b_o, v_o, bonus_o, g_o, *, boundary):
    p = p_ref[...]
    rows = p.shape[0]
    if boundary:
        row_id = lax.broadcasted_iota(jnp.int32, (rows, 1), 0)
        prev = jnp.where(row_id == 0, prev_ref[0:1, :], pltpu.roll(p, 1, 0))
    else:
        prev = prev_ref[...]
    xs = p + (prev - p) * mu_ref[...]
    r = xs[:, 0:D_RWKV]
    k = xs[:, D_RWKV:2 * D_RWKV]
    v = xs[:, 2 * D_RWKV:3 * D_RWKV]
    lo = xs[:, 3 * D_RWKV:]
    mm = lambda x, w_ref_: jnp.dot(x.astype(BF16), w_ref_[...], preferred_element_type=F32)
    w = -jax.nn.softplus(-(w0_ref[...] + mm(jnp.tanh(lo), wl_ref))) - 0.5
    a = jax.nn.sigmoid(a0_ref[...] + mm(lo, al_ref))
    def put(o_ref, q, x):
        o_ref[2 * q] = x[:, :LANES]
        o_ref[2 * q + 1] = x[:, LANES:]

    g_o[...] = mm(jax.nn.sigmoid(lo), gl_ref)
    decay = jnp.exp(-jnp.exp(w))
    k_mod = k * (1.0 + (a - 1.0) * ka_ref[...])
    kk = k * kk_scale_ref[...]
    rk = r * k_mod * rk_ref[...]
    v_o[...] = v
    for q in range(RW_NQ):
        sl = slice(q * RW_QUAD, (q + 1) * RW_QUAD)
        kq = kk[:, sl]
        norm = jnp.maximum(jnp.sqrt(_seg_sum(kq * kq, ones2_ref[...])), 1e-12)
        kq = kq / norm
        put(kk_o, q, kq)
        put(b_o, q, kq * a[:, sl])
        vq = v[:, sl]
        bonus_o[:, sl] = _seg_sum(rk[:, sl], ones2_ref[...]) * vq
        put(w_o, q, decay[:, sl])
        put(r_o, q, r[:, sl])
        put(k_o, q, k_mod[:, sl])


def rwkv_pre(p_a, prev, boundary, T, W, *, tr=128):
    m = p_a.shape[0]
    tr = min(tr, m)
    assert (not boundary or T % tr == 0) and m % tr == 0
    padc = lambda t: jnp.pad(t, (0, RW_COLS_PAD - RW_COLS)).reshape(1, RW_COLS_PAD)
    vec = lambda t: t.reshape(1, D_RWKV)

    def lora_pad(w_up, row0):
        return jnp.zeros((LORA_COLS, D_RWKV), BF16).at[row0:row0 + w_up.shape[0]].set(w_up.astype(BF16))

    ones = _block_diag_ones(BF16)
    ones2 = jnp.concatenate([ones, ones], axis=0)
    row = pl.BlockSpec((tr, RW_COLS_PAD), lambda i: (i, 0))
    prev_spec = pl.BlockSpec((8, RW_COLS_PAD), lambda i: (i, 0)) if boundary else row
    const = lambda shape: pl.BlockSpec(shape, lambda i: (0,) * len(shape))
    out = pl.BlockSpec((tr, D_RWKV), lambda i: (i, 0))
    nlb = D_RWKV // LANES
    out_lb = pl.BlockSpec((nlb, tr, LANES), lambda i: (0, i, 0))
    return pl.pallas_call(
        functools.partial(_rwkv_pre_kernel, boundary=boundary),
        grid=(m // tr,),
        in_specs=[row, prev_spec, const((1, RW_COLS_PAD))] + [const((1, D_RWKV))] * 5
                 + [const((LORA_COLS, D_RWKV))] * 3 + [const((2 * RW_QUAD, RW_QUAD))],
        out_specs=[out_lb] * 5 + [out] * 3,
        out_shape=[jax.ShapeDtypeStruct((nlb, m, LANES), F32)] * 5 + [jax.ShapeDtypeStruct((m, D_RWKV), F32)] * 3,
        compiler_params=_cparams(("parallel",)),
        name="rwkv_pre",
    )(p_a, prev, padc(W['mu_shift']), vec(W['w0']), vec(W['a0']), vec(W['k_k']), vec(W['k_a']), vec(W['r_k']),
      lora_pad(W['w_lora_up'], 0), lora_pad(W['a_lora_up'], DECAY_LORA),
      lora_pad(W['g_lora_up'], DECAY_LORA + AAA_LORA), ones2)


SCAN_BATCH = 4


def _block_diag_ones(dtype):
    r2 = lax.broadcasted_iota(jnp.int32, (RW_QUAD, RW_QUAD), 0) // RW_HEAD
    c2 = lax.broadcasted_iota(jnp.int32, (RW_QUAD, RW_QUAD), 1) // RW_HEAD
    return (r2 == c2).astype(dtype)


def _seg_sum(x, ones2):
    hi = x.astype(BF16)
    lo = (x - hi.astype(F32)).astype(BF16)
    return jnp.dot(jnp.concatenate([hi, lo], axis=1), ones2, preferred_element_type=F32)


def _repeat_row(ref, lb, i, t):
    return ref[lb, i, pl.ds(t, RW_HEAD, stride=0), :]


def _seg_sum_xlane(x):
    first = lax.broadcasted_iota(jnp.int32, (1, LANES), 1) < RW_HEAD
    cols = []
    for c in range(x.shape[1] // LANES):
        xc = x[:, c * LANES:(c + 1) * LANES]
        sa = jnp.sum(jnp.where(first, xc, 0.0), axis=-1, keepdims=True)
        sb = jnp.sum(jnp.where(first, 0.0, xc), axis=-1, keepdims=True)
        cols.append(jnp.where(first, sa, sb))
    return jnp.concatenate(cols, axis=1)


V_HALF = 32


def _scan_kernel(w_ref, r_ref, k_ref, kk_ref, b_ref, v_ref, bonus_ref, g_ref, s0_ref,
                 tsel_ref, ones_ref, ones2_ref, lnw_ref, lnb_ref,
                 o_ref, s_ref, y_ref, nat_ref, vt_ref, vtmp_ref, *, tc, nb):
    c = pl.program_id(1)
    n_half = (tc + V_HALF - 1) // V_HALF

    @pl.when(c == 0)
    def _():
        s_ref[...] = s0_ref[...]

    y_ref[...] = jnp.zeros(y_ref.shape, F32)
    lane_t = lax.broadcasted_iota(jnp.int32, (1, RW_QUAD), 1) % RW_HEAD

    for i in range(nb):
        v = v_ref[i]
        if tc < LANES:
            v = jnp.concatenate([v, jnp.zeros((LANES - tc, D_RWKV), F32)], axis=0)
        vt = v.T
        hi = vt.astype(BF16).astype(F32)
        pieces = (hi, vt - hi)
        for a in range(n_half):
            for q in range(RW_NQ):
                for pc in range(2):
                    for h4 in range(4):
                        src = pieces[pc][(q * 4 + h4) * RW_HEAD:(q * 4 + h4 + 1) * RW_HEAD, a * V_HALF:(a + 1) * V_HALF]
                        dst = (pc * 4 + h4) * V_HALF
                        vtmp_ref[q * RW_HEAD:(q + 1) * RW_HEAD, dst:dst + V_HALF] = src
            vt_ref[a, i * RW_ROWS:(i + 1) * RW_ROWS, :] = vtmp_ref[...].astype(BF16)

    def stack(ref, i, t):
        rep = lambda lb: _repeat_row(ref, lb, i, t)
        return jnp.concatenate(
            [jnp.concatenate([rep(2 * q), rep(2 * q + 1)], axis=1) for q in range(RW_NQ)], axis=0)

    def body(t, carry, a):
        sel = jnp.where(tsel_ref[...] == t - a * V_HALF, 1.0, 0.0).astype(BF16)
        rows = lambda ref: jnp.concatenate([stack(ref, i, t) for i in range(nb)], axis=0)
        s = jnp.concatenate([s_ref[i] for i in range(nb)], axis=0)
        p = s * rows(kk_ref)
        xl = nb * RW_ROWS // 2
        s_kk = jnp.concatenate([_seg_sum_xlane(p[:xl]), _seg_sum(p[xl:], ones2_ref[...])], axis=0)
        v_col = jnp.dot(vt_ref[a], sel, preferred_element_type=F32)
        s = s * rows(w_ref) - s_kk * rows(b_ref) + v_col * rows(k_ref)
        for i in range(nb):
            s_ref[i] = s[i * RW_ROWS:(i + 1) * RW_ROWS]
        y_b = jnp.dot((s * rows(r_ref)).astype(BF16), ones_ref[...], preferred_element_type=F32)
        y_ref[...] = jnp.where(lane_t == t, y_b, y_ref[...])
        return carry

    for a in range(n_half):
        lax.fori_loop(a * V_HALF, min(tc, (a + 1) * V_HALF), functools.partial(body, a=a), 0, unroll=2)

    for i in range(nb):
        for q in range(RW_NQ):
            rq = slice(i * RW_ROWS + q * RW_HEAD, i * RW_ROWS + (q + 1) * RW_HEAD)
            yq = y_ref[rq, :]
            mu = jnp.mean(yq, axis=0, keepdims=True)
            d = yq - mu
            var = jnp.mean(d * d, axis=0, keepdims=True)
            y_ref[rq, :] = d * lax.rsqrt(var + GN_EPS)
        yt = y_ref[i * RW_ROWS:(i + 1) * RW_ROWS, :].T
        for q in range(RW_NQ):
            for h4 in range(4):
                ch = (q * 4 + h4) * RW_HEAD
                nat_ref[:, ch:ch + RW_HEAD] = yt[h4 * RW_HEAD:h4 * RW_HEAD + tc, q * RW_HEAD:(q + 1) * RW_HEAD]
        o = (nat_ref[...] * lnw_ref[...] + lnb_ref[...] + bonus_ref[i]) * g_ref[i]
        o_ref[i] = o.astype(o_ref.dtype)


def rwkv_scan(w, r, k, kk, b, v, bonus, g, s0, ln_w, ln_b):
    B, T, _ = bonus.shape
    tc = min(Y_TILE, T)
    nc = T // tc
    nb = SCAN_BATCH
    assert T % tc == 0 and B % nb == 0
    s0q = s0.reshape(B, RW_NQ, 4, RW_HEAD, RW_HEAD).transpose(0, 1, 3, 2, 4).reshape(B, RW_ROWS, RW_QUAD)
    rr = lax.broadcasted_iota(jnp.int32, (RW_QUAD, RW_QUAD), 0)
    cc = lax.broadcasted_iota(jnp.int32, (RW_QUAD, RW_QUAD), 1)
    tsel = jnp.where((rr // V_HALF) % 4 == cc // RW_HEAD, rr % V_HALF, -1)
    ones = _block_diag_ones(BF16)
    ones2 = jnp.concatenate([ones, ones], axis=0)
    xspec = pl.BlockSpec((nb, tc, D_RWKV), lambda bi, ci: (bi, ci, 0))
    sspec = pl.BlockSpec((nb, RW_ROWS, RW_QUAD), lambda bi, ci: (bi, 0, 0))
    const = lambda shape: pl.BlockSpec(shape, lambda bi, ci: (0,) * len(shape))
    o, s_fin = pl.pallas_call(
        functools.partial(_scan_kernel, tc=tc, nb=nb),
        grid=(B // nb, nc),
        in_specs=[pl.BlockSpec((D_RWKV // LANES, nb, tc, LANES), lambda bi, ci: (0, bi, ci, 0))] * 5
                 + [xspec] * 3 + [sspec, const((RW_QUAD, RW_QUAD)),
                                const((RW_QUAD, RW_QUAD)), const((2 * RW_QUAD, RW_QUAD)),
                                const((1, D_RWKV)), const((1, D_RWKV))],
        out_specs=[xspec, sspec],
        out_shape=[jax.ShapeDtypeStruct((B, T, D_RWKV), BF16),
                   jax.ShapeDtypeStruct((B, RW_ROWS, RW_QUAD), F32)],
        scratch_shapes=[pltpu.VMEM((nb * RW_ROWS, RW_QUAD), F32), pltpu.VMEM((tc, D_RWKV), F32),
                        pltpu.VMEM(((tc + V_HALF - 1) // V_HALF, nb * RW_ROWS, RW_QUAD), BF16),
                        pltpu.VMEM((RW_ROWS, RW_QUAD), F32)],
        compiler_params=_cparams(("parallel", "arbitrary")),
        name="rwkv_scan",
    )(w, r, k, kk, b, v, bonus, g, s0q, tsel, ones, ones2,
      ln_w.reshape(1, D_RWKV), ln_b.reshape(1, D_RWKV))
    s_fin = s_fin.reshape(B, RW_NQ, RW_HEAD, 4, RW_HEAD).transpose(0, 1, 3, 2, 4).reshape(B, RW_HEADS, RW_HEAD, RW_HEAD)
    return o, s_fin


ATT_SCALE = ATT_QK ** -0.5


def _lam_from(lamp_ref):
    lp = lamp_ref[...]
    return (jnp.exp(jnp.sum(lp[0:1] * lp[1:2])) - jnp.exp(jnp.sum(lp[2:3] * lp[3:4])) + LAM_INIT)


def _online_softmax_step(s, v_bf, m, l, a):
    m_new = jnp.maximum(m, jnp.max(s, axis=-1, keepdims=True))
    alpha = jnp.exp(m - m_new)
    p = jnp.exp(s - m_new)
    l = alpha * l + jnp.sum(p, axis=-1, keepdims=True)
    a = alpha * a + jnp.dot(p.astype(BF16), v_bf, preferred_element_type=F32)
    return m_new, l, a


def _subln(o, g):
    ms = jnp.mean(o * o, axis=-1, keepdims=True)
    return o * lax.rsqrt(ms + NORM_EPS) * g * (1.0 - LAM_INIT)


def _diff_prompt_kernel(slopes_ref, q_ref, k_ref, v_ref, lamp_ref, g_ref, o_ref, *, tq, tk):
    h = pl.program_id(1)
    qi = pl.program_id(2)
    slope = slopes_ref[h]
    lam = _lam_from(lamp_ref)
    q = q_ref[...]
    q1 = q[:, :ATT_QK].astype(BF16)
    q2 = q[:, ATT_QK:].astype(BF16)
    rel = (lax.broadcasted_iota(jnp.int32, (tq, tk), 0) - lax.broadcasted_iota(jnp.int32, (tq, tk), 1))
    base = -slope * rel.astype(F32)

    def kv_step(kb, carry, diagonal=False):
        m1, l1, a1, m2, l2, a2 = carry
        k0 = pl.multiple_of(kb * tk, tk)
        k = k_ref[pl.ds(k0, tk), :]
        v = v_ref[pl.ds(k0, tk), :].astype(BF16)
        bias = base - slope * (qi * tq - k0).astype(F32)
        if diagonal:
            bias = jnp.where(rel >= 0, bias, NEG_INF)
        s1 = lax.dot_general(q1, k[:, :ATT_QK].astype(BF16), DOT_NT, preferred_element_type=F32) * ATT_SCALE + bias
        s2 = lax.dot_general(q2, k[:, ATT_QK:].astype(BF16), DOT_NT, preferred_element_type=F32) * ATT_SCALE + bias
        m1, l1, a1 = _online_softmax_step(s1, v, m1, l1, a1)
        m2, l2, a2 = _online_softmax_step(s2, v, m2, l2, a2)
        return m1, l1, a1, m2, l2, a2

    col = lambda val: jnp.full((tq, 1), val, F32)
    acc0 = jnp.zeros((tq, ATT_V), F32)
    carry = lax.fori_loop(0, qi, kv_step, (col(NEG_INF), col(0.0), acc0, col(NEG_INF), col(0.0), acc0))
    m1, l1, a1, m2, l2, a2 = kv_step(qi, carry, diagonal=True)
    o = a1 / l1 - lam * (a2 / l2)
    o_ref[...] = _subln(o, g_ref[...]).astype(o_ref.dtype)


def diff_attn_prompt(q, k, v, B, T, slopes, lamp, g_subln, *, tq=512, tk=512):
    assert tq == tk and T % tq == 0
    nq = T // tq
    return pl.pallas_call(
        functools.partial(_diff_prompt_kernel, tq=tq, tk=tk),
        grid=(B, ATT_HEADS, nq),
        in_specs=[pl.BlockSpec(memory_space=pltpu.SMEM),
                  pl.BlockSpec((tq, ATT_V), lambda b, h, i: (b * nq + i, h)),
                  pl.BlockSpec((T, ATT_V), lambda b, h, i: (b, h)),
                  pl.BlockSpec((T, ATT_V), lambda b, h, i: (b, h)),
                  pl.BlockSpec((4, ATT_QK), lambda b, h, i: (0, 0)),
                  pl.BlockSpec((1, ATT_V), lambda b, h, i: (0, 0))],
        out_specs=pl.BlockSpec((tq, ATT_V), lambda b, h, i: (b * nq + i, h)),
        out_shape=jax.ShapeDtypeStruct((B * T, D_ATT), BF16),
        compiler_params=_cparams(("parallel", "parallel", "arbitrary")),
        name="diff_attn_prompt",
    )(slopes, q, k, v, lamp, g_subln.reshape(1, ATT_V))


PAGES_PER_STEP = 4
QROWS = 16


PAGE_ROWS = PAGE_SIZE * ATT_HEADS
DOT_TN = (((0,), (0,)), ((), ()))


def _diff_sample_kernel(pt_ref, slopes_ref, q_ref, kn_ref, vn_ref, lamp_ref, g_ref, *rest, ds, past):
    k_refs = rest[:PAGES_PER_STEP]
    v_refs = rest[PAGES_PER_STEP:2 * PAGES_PER_STEP]
    o_ref = rest[2 * PAGES_PER_STEP]
    qw_ref, bias_ref, m_ref, l_ref, a_ref = rest[2 * PAGES_PER_STEP + 1:]
    j = pl.program_id(1)
    cph = 2 * ds
    col = lax.broadcasted_iota(jnp.int32, (1, LANES), 1)
    col_head = col // cph
    col_q = col % ds
    slope_col = jnp.zeros((1, LANES), F32)
    for h in range(ATT_HEADS):
        slope_col = jnp.where(col_head == h, slopes_ref[h], slope_col)
    row_key = lax.broadcasted_iota(jnp.int32, (PAGE_ROWS, 1), 0) // ATT_HEADS
    row_head = lax.broadcasted_iota(jnp.int32, (PAGE_ROWS, 1), 0) % ATT_HEADS

    @pl.when(j == 0)
    def _():
        lane = lax.broadcasted_iota(jnp.int32, (ds, ATT_V), 1)
        rows = []
        for h in range(ATT_HEADS):
            qh = q_ref[0, :, h * ATT_V:(h + 1) * ATT_V]
            rows += [jnp.where(lane < ATT_QK, qh, 0.0), jnp.where(lane >= ATT_QK, qh, 0.0)]
        rows.append(jnp.zeros((LANES - ATT_HEADS * cph, ATT_V), F32))
        qw_ref[...] = jnp.concatenate(rows, axis=0).T.astype(BF16)
        bias_ref[...] = jnp.where(row_head == col_head, slope_col * row_key.astype(F32), NEG_INF)
        m_ref[...] = jnp.full(m_ref.shape, NEG_INF, F32)
        l_ref[...] = jnp.zeros(l_ref.shape, F32)
        a_ref[...] = jnp.zeros(a_ref.shape, F32)

    eye = (lax.broadcasted_iota(jnp.int32, (LANES, LANES), 0) == lax.broadcasted_iota(jnp.int32, (LANES, LANES), 1))
    to_col = lambda r: jnp.sum(jnp.where(eye, r, 0.0), axis=1, keepdims=True)

    def attend(k_list, v_list, key0, extra):
        s_pages = []
        for n, k_ref in enumerate(k_list):
            k2 = k_ref[0].reshape(PAGE_ROWS, ATT_V).astype(BF16)
            s = jnp.dot(k2, qw_ref[...], preferred_element_type=F32) * ATT_SCALE + bias_ref[...]
            s_pages.append(s + slope_col * (key0 + n * PAGE_SIZE - past - col_q).astype(F32))
        s = jnp.concatenate(s_pages, axis=0)
        if extra is not None:
            s = jnp.where(extra, s, NEG_INF)
        v2 = jnp.concatenate([v_ref[0].reshape(PAGE_ROWS, ATT_V).astype(BF16) for v_ref in v_list], axis=0)
        m_old = m_ref[...]
        m_new = jnp.maximum(m_old, jnp.max(s, axis=0, keepdims=True))
        alpha = jnp.exp(m_old - m_new)
        p = jnp.exp(s - m_new)
        l_ref[...] = alpha * l_ref[...] + jnp.sum(p, axis=0, keepdims=True)
        m_ref[...] = m_new
        pv = lax.dot_general(p.astype(BF16), v2, DOT_TN, preferred_element_type=F32)
        a_ref[...] = a_ref[...] * to_col(alpha) + pv

    attend(k_refs, v_refs, j * PAGES_PER_STEP * PAGE_SIZE, None)

    @pl.when(j == pl.num_programs(1) - 1)
    def _():
        attend([kn_ref], [vn_ref], past, (row_key < ds) & (row_key <= col_q))
        lam = _lam_from(lamp_ref)
        o_all = a_ref[...] / to_col(l_ref[...])
        for h in range(ATT_HEADS):
            o = o_all[h * cph:h * cph + ds] - lam * o_all[h * cph + ds:(h + 1) * cph]
            o_ref[0, :, h * ATT_V:(h + 1) * ATT_V] = _subln(o, g_ref[...]).astype(o_ref.dtype)


def diff_attn_sample(q, k_new, v_new, cache_k, cache_v, page_table, slopes, lamp, g_subln):
    B, ds, _ = q.shape
    n_pages = page_table.shape[1]
    past = n_pages * PAGE_SIZE
    assert n_pages % PAGES_PER_STEP == 0 and ATT_HEADS * 2 * ds <= LANES
    pad = lambda t: jnp.pad(t, ((0, 0), (0, PAGE_SIZE - ds), (0, 0), (0, 0)))
    page_blk = (1, PAGE_SIZE, ATT_HEADS, ATT_V)

    def page_spec(i):
        return pl.BlockSpec(page_blk, lambda b, j, pt: (pt[b, j * PAGES_PER_STEP + i], 0, 0, 0))

    fixed = lambda shape: pl.BlockSpec(shape, lambda b, j, pt: (0,) * len(shape))
    grid_spec = pltpu.PrefetchScalarGridSpec(
        num_scalar_prefetch=1,
        grid=(B, n_pages // PAGES_PER_STEP),
        in_specs=[pl.BlockSpec(memory_space=pltpu.SMEM),
                  pl.BlockSpec((1, ds, D_ATT), lambda b, j, pt: (b, 0, 0)),
                  pl.BlockSpec(page_blk, lambda b, j, pt: (b, 0, 0, 0)),
                  pl.BlockSpec(page_blk, lambda b, j, pt: (b, 0, 0, 0)),
                  fixed((4, ATT_QK)), fixed((1, ATT_V))]
                 + [page_spec(i) for i in range(PAGES_PER_STEP)] * 2,
        out_specs=pl.BlockSpec((1, ds, D_ATT), lambda b, j, pt: (b, 0, 0)),
        scratch_shapes=[pltpu.VMEM((ATT_V, LANES), BF16),
                        pltpu.VMEM((PAGE_ROWS, LANES), F32),
                        pltpu.VMEM((1, LANES), F32),
                        pltpu.VMEM((1, LANES), F32),
                        pltpu.VMEM((LANES, ATT_V), F32)])
    return pl.pallas_call(
        functools.partial(_diff_sample_kernel, ds=ds, past=past),
        grid_spec=grid_spec,
        out_shape=jax.ShapeDtypeStruct((B, ds, D_ATT), BF16),
        compiler_params=_cparams(("parallel", "arbitrary")),
        name="diff_attn_sample",
    )(page_table, slopes, q, pad(k_new), pad(v_new), lamp, g_subln.reshape(1, ATT_V),
      *([cache_k] * PAGES_PER_STEP), *([cache_v] * PAGES_PER_STEP))


def _mem_attn_kernel(q_ref, k_ref, v_ref, o_ref, *, paged_heads, rows):
    for h in range(MEM_HEADS):
        sl = slice(h * MEM_HEAD, (h + 1) * MEM_HEAD)
        q = q_ref[:, sl] if q_ref.ndim == 2 else q_ref[0, :, sl]
        if q.shape[0] < rows:
            q = jnp.concatenate([q, jnp.zeros((rows - q.shape[0], MEM_HEAD), F32)], axis=0)
        k = k_ref[0, :, h, :] if paged_heads else k_ref[:, sl]
        v = v_ref[0, :, h, :] if paged_heads else v_ref[:, sl]
        s = lax.dot_general(q.astype(BF16), k.astype(BF16), DOT_NT, preferred_element_type=F32) * (MEM_HEAD ** -0.5)
        p = jnp.exp(s - jnp.max(s, axis=-1, keepdims=True))
        o = jnp.dot(p.astype(BF16), v.astype(BF16), preferred_element_type=F32) / jnp.sum(p, axis=-1, keepdims=True)
        if q_ref.ndim == 2:
            o_ref[:, sl] = o.astype(o_ref.dtype)
        else:
            o_ref[0, :, sl] = o[:q_ref.shape[1]].astype(o_ref.dtype)


def mem_attn_prompt(p_c, kv, B, T, MT, *, tq=512):
    tq = min(tq, T)
    nq = T // tq
    return pl.pallas_call(
        functools.partial(_mem_attn_kernel, paged_heads=False, rows=tq),
        grid=(B, nq),
        in_specs=[pl.BlockSpec((tq, D_MEM), lambda b, i: (b * nq + i, 0)),
                  pl.BlockSpec((MT, D_MEM), lambda b, i: (b, 0)),
                  pl.BlockSpec((MT, D_MEM), lambda b, i: (b, 1))],
        out_specs=pl.BlockSpec((tq, D_MEM), lambda b, i: (b * nq + i, 0)),
        out_shape=jax.ShapeDtypeStruct((B * T, D_MEM), BF16),
        compiler_params=_cparams(("parallel", "parallel")),
        name="mem_attn_prompt",
    )(p_c, kv, kv)


def mem_attn_sample(qm, mem_k, mem_v):
    B, ds, _ = qm.shape
    MT = mem_k.shape[1]
    kv_spec = pl.BlockSpec((1, MT, MEM_HEADS, MEM_HEAD), lambda b: (b, 0, 0, 0))
    return pl.pallas_call(
        functools.partial(_mem_attn_kernel, paged_heads=True, rows=QROWS),
        grid=(B,),
        in_specs=[pl.BlockSpec((1, ds, D_MEM), lambda b: (b, 0, 0)), kv_spec, kv_spec],
        out_specs=pl.BlockSpec((1, ds, D_MEM), lambda b: (b, 0, 0)),
        out_shape=jax.ShapeDtypeStruct((B, ds, D_MEM), BF16),
        compiler_params=_cparams(("parallel",)),
        name="mem_attn_sample",
    )(qm, mem_k, mem_v)


PRE_ROWS = 128


def _rwkv7_branch(p_a, B, T, shift_prev, s0, W):
    M = B * T
    sp = jnp.pad(shift_prev, ((0, 0), (0, RW_COLS_PAD - RW_COLS)))
    if T % PRE_ROWS == 0:
        first = jnp.arange(M // PRE_ROWS) * PRE_ROWS
        rows = jnp.where((first % T == 0)[:, None], sp[first // T], p_a[jnp.maximum(first - 1, 0)])
        prev = jnp.zeros((M // PRE_ROWS, 8, RW_COLS_PAD), F32).at[:, 0].set(rows).reshape(-1, RW_COLS_PAD)
        boundary = True
    else:
        p3 = p_a.reshape(B, T, RW_COLS_PAD)
        prev = jnp.concatenate([sp[:, None], p3[:, :-1]], axis=1).reshape(M, RW_COLS_PAD)
        boundary = False
    outs = rwkv_pre(p_a, prev, boundary, T, W, tr=PRE_ROWS)
    per_step = [t.reshape(D_RWKV // LANES, B, T, LANES) for t in outs[:5]]
    v, bonus, g = (t.reshape(B, T, D_RWKV) for t in outs[5:])
    o, s_fin = rwkv_scan(*per_step, v, bonus, g, s0, W['ln_x_w'], W['ln_x_b'])
    shift_last = p_a.reshape(B, T, RW_COLS_PAD)[:, -1, :RW_COLS]
    return o.reshape(M, D_RWKV), s_fin, shift_last


def _layer(x, shift_prev, s0, attend, mem_attend, W):
    B, T, _ = x.shape
    M = B * T
    x2 = x.reshape(M, D_MODEL)
    h = rmsnorm_bf16(x2, W['g_pre_mix'])
    wt = W['w_in_t']
    p_a = matmul_nt(h, wt, n_cols=RW_COLS_PAD)
    q = matmul_nt(h, wt, row0=RW_COLS, n_cols=D_ATT)
    k = matmul_nt(h, wt, row0=RW_COLS + D_ATT, n_cols=D_ATT)
    v = matmul_nt(h, wt, row0=RW_COLS + 2 * D_ATT, n_cols=D_ATT)
    p_c = matmul_nt(h, wt, row0=RW_COLS + 3 * D_ATT, n_cols=D_MEM + 3 * D_MODEL)

    o_rw, s_fin, shift_last = _rwkv7_branch(p_a, B, T, shift_prev, s0, W)
    o_att = attend(q, k, v)
    o_mem = mem_attend(p_c)

    m = gated_merge(o_rw, o_att, o_mem,
                    W['w_br_rwkv'], W['w_br_attn'], W['w_br_mem'], p_c, D_MEM)
    k = k.reshape(B, T, ATT_HEADS, 2 * ATT_QK)
    v = v.reshape(B, T, ATT_HEADS, ATT_V)
    z = matmul(m, W['w_out'])
    x1, h2 = norm_residual(z, W['g_post_mix'], x2, W['g_pre_mlp'])
    u = matmul(h2, W['w_up'], out_dtype=BF16, relu2=True)
    z2 = matmul(u, W['w_down'], tm=2048, tk=2048)
    y = norm_residual(z2, W['g_post_mlp'], x1)
    return y.reshape(B, T, D_MODEL), k, v, s_fin, shift_last


def kernel(x_prompt, x_sample, cache_k, cache_v, page_table, state_wkv, state_shift, cache_mem_k, cache_mem_v, mem_prompt, g_pre_mix, w_in, mu_shift, w0, w_lora_up, a0, a_lora_up, g_lora_up, k_k, k_a, r_k, ln_x_w, ln_x_b, lambda_q1, lambda_k1, lambda_q2, lambda_k2, g_subln, g_mem, w_mem_kv, w_br_rwkv, w_br_attn, w_br_mem, w_out, g_post_mix, g_pre_mlp, w_up, w_down, g_post_mlp):
    W = dict(g_pre_mix=g_pre_mix, w_in_t=w_in.T,
             mu_shift=mu_shift, w0=w0, w_lora_up=w_lora_up,
             a0=a0, a_lora_up=a_lora_up, g_lora_up=g_lora_up, k_k=k_k, k_a=k_a, r_k=r_k,
             ln_x_w=ln_x_w, ln_x_b=ln_x_b, lambda_q1=lambda_q1, lambda_k1=lambda_k1,
             lambda_q2=lambda_q2, lambda_k2=lambda_k2, g_subln=g_subln, w_br_rwkv=w_br_rwkv.astype(BF16),
             w_br_attn=w_br_attn.astype(BF16), w_br_mem=w_br_mem.astype(BF16), w_out=w_out,
             g_post_mix=g_post_mix,
             g_pre_mlp=g_pre_mlp, w_up=w_up, w_down=w_down, g_post_mlp=g_post_mlp)
    slopes = jnp.asarray([2.0 ** (-8.0 * (h + 1) / ATT_HEADS) for h in range(ATT_HEADS)], dtype=F32)
    lamp = jnp.stack([lambda_q1, lambda_k1, lambda_q2, lambda_k2])
    B, T, _ = x_prompt.shape
    BS, DS, _ = x_sample.shape
    MT = mem_prompt.shape[1]

    hm = rmsnorm_bf16(mem_prompt.reshape(B * MT, D_MODEL), g_mem)
    kv = matmul(hm, w_mem_kv)
    mem_k_prompt = kv[:, :D_MEM].reshape(B, MT, MEM_HEADS, MEM_HEAD)
    mem_v_prompt = kv[:, D_MEM:].reshape(B, MT, MEM_HEADS, MEM_HEAD)

    y_prompt, k_prompt, v_prompt, wkv_prompt, shift_prompt = _layer(
        x_prompt, jnp.zeros((B, RW_COLS), F32), jnp.zeros((B, RW_HEADS, RW_HEAD, RW_HEAD), F32),
        lambda q, k, v: diff_attn_prompt(q, k, v, B, T, slopes, lamp, g_subln),
        lambda p_c: mem_attn_prompt(p_c, kv, B, T, MT), W)

    def attend_sample(q, k, v):
        page = lambda t: t.reshape(BS, DS, ATT_HEADS, ATT_V)
        o = diff_attn_sample(q.reshape(BS, DS, D_ATT), page(k), page(v), cache_k, cache_v, page_table,
                             slopes, lamp, g_subln)
        return o.reshape(BS * DS, D_ATT)

    def mem_attend_sample(p_c):
        qm = p_c[:, :D_MEM].reshape(BS, DS, D_MEM)
        return mem_attn_sample(qm, cache_mem_k, cache_mem_v).reshape(BS * DS, D_MEM)

    y_sample, k_sample, v_sample, wkv_sample, shift_sample = _layer(
        x_sample, state_shift, state_wkv, attend_sample, mem_attend_sample, W)
    return (y_prompt, y_sample, k_prompt, v_prompt, k_sample, v_sample,
            wkv_prompt, wkv_sample, shift_prompt, shift_sample, mem_k_prompt, mem_v_prompt)
```

```python
import functools
import math

import jax
import jax.numpy as jnp
from jax import lax
from jax.experimental import pallas as pl
from jax.experimental.pallas import tpu as pltpu

F32 = jnp.float32
BF16 = jnp.bfloat16

D_MODEL = 4096
PAGE_SIZE = 128
D_RWKV = 2048
RW_HEAD = 64
RW_HEADS = 32
DECAY_LORA = 96
AAA_LORA = 96
GATE_LORA = 256
RW_COLS = 3 * D_RWKV + DECAY_LORA + AAA_LORA + GATE_LORA
RW_COLS_PAD = 6656
GN_EPS = 64e-5
D_ATT = 2048
ATT_HEADS = 8
ATT_QK = 128
ATT_V = 256
Q_BLOCK = 128
MEM_HEADS = 4
MEM_HEAD = 128
D_MEM = 512
D_FF = 4 * D_MODEL
NORM_EPS = 1e-6
NEG_INF = -1e30
LAM_INIT = 0.8 - 0.6 * math.exp(-0.3 * 0)

VMEM_LIMIT = 56 * 1024 * 1024
LANES = 128


def _cparams(sem):
    return pltpu.CompilerParams(dimension_semantics=sem, vmem_limit_bytes=VMEM_LIMIT)


def _rmsnorm_kernel(x_ref, g_ref, o_ref):
    x = x_ref[...]
    ms = jnp.mean(x * x, axis=-1, keepdims=True)
    o_ref[...] = (x * lax.rsqrt(ms + NORM_EPS) * g_ref[...]).astype(o_ref.dtype)


def rmsnorm_bf16(x, g, tr=256):
    m, d = x.shape
    tr = min(tr, m)
    return pl.pallas_call(
        _rmsnorm_kernel,
        grid=(m // tr,),
        in_specs=[pl.BlockSpec((tr, d), lambda i: (i, 0)),
                  pl.BlockSpec((1, d), lambda i: (0, 0))],
        out_specs=pl.BlockSpec((tr, d), lambda i: (i, 0)),
        out_shape=jax.ShapeDtypeStruct((m, d), BF16),
        compiler_params=_cparams(("parallel",)),
        name="rmsnorm_bf16",
    )(x, g.reshape(1, d))


DOT_NT = (((1,), (1,)), ((), ()))


def _mm_nt_kernel(a_ref, bt_ref, o_ref):
    o_ref[...] = lax.dot_general(a_ref[...], bt_ref[...].astype(BF16), DOT_NT,
                                 preferred_element_type=F32).astype(o_ref.dtype)


def matmul_nt(a, bt, *, row0=0, n_cols, tm=2048, tn=512):
    m, kd = a.shape
    tm = min(tm, m)
    assert m % tm == 0 and n_cols % tn == 0 and row0 % 8 == 0 and bt.shape[1] == kd
    return pl.pallas_call(
        _mm_nt_kernel,
        grid=(m // tm, n_cols // tn),
        in_specs=[pl.BlockSpec((tm, kd), lambda i, j: (i, 0), pipeline_mode=pl.Buffered(1)),
                  pl.BlockSpec((pl.Element(tn), pl.Element(kd)),
                               lambda i, j: ((row0 // 8 + j * (tn // 8)) * 8, 0))],
        out_specs=pl.BlockSpec((tm, tn), lambda i, j: (i, j)),
        out_shape=jax.ShapeDtypeStruct((m, n_cols), F32),
        compiler_params=_cparams(("parallel", "parallel")),
        name="matmul_nt",
    )(a, bt)


def _mm_kernel(a_ref, b_ref, o_ref, *scratch, relu2, nk):
    def finish(r):
        if relu2:
            r = jnp.square(jnp.maximum(r, 0.0))
        o_ref[...] = r.astype(o_ref.dtype)

    part = jnp.dot(a_ref[...], b_ref[...].astype(BF16), preferred_element_type=F32)
    if nk == 1:
        finish(part)
        return
    acc_ref, = scratch
    k = pl.program_id(2)

    @pl.when(k == 0)
    def _():
        acc_ref[...] = part

    @pl.when(k > 0)
    def _():
        acc_ref[...] += part

    @pl.when(k == nk - 1)
    def _():
        finish(acc_ref[...])


def matmul(a, b, *, tm=2048, tn=512, tk=4096, out_dtype=F32, relu2=False):
    m, kd = a.shape
    n = b.shape[1]
    tm = min(tm, m)
    tk = min(tk, kd)
    assert m % tm == 0 and n % tn == 0 and kd % tk == 0
    nk = kd // tk
    a_mode = dict(pipeline_mode=pl.Buffered(1)) if nk == 1 else {}
    return pl.pallas_call(
        functools.partial(_mm_kernel, relu2=relu2, nk=nk),
        grid=(m // tm, n // tn, nk),
        in_specs=[pl.BlockSpec((tm, tk), lambda i, j, k: (i, k), **a_mode),
                  pl.BlockSpec((tk, tn), lambda i, j, k: (k, j))],
        out_specs=pl.BlockSpec((tm, tn), lambda i, j, k: (i, j)),
        out_shape=jax.ShapeDtypeStruct((m, n), out_dtype),
        scratch_shapes=[pltpu.VMEM((tm, tn), F32)] if nk > 1 else [],
        compiler_params=_cparams(("parallel", "parallel", "arbitrary")),
        name="matmul",
    )(a, b)


def _norm_res_kernel(z_ref, g_ref, x_ref, *rest, with_next):
    z = z_ref[...]
    ms = jnp.mean(z * z, axis=-1, keepdims=True)
    y = x_ref[...] + z * lax.rsqrt(ms + NORM_EPS) * g_ref[...]
    if with_next:
        g2_ref, o_ref, h_ref = rest
        ms2 = jnp.mean(y * y, axis=-1, keepdims=True)
        h_ref[...] = (y * lax.rsqrt(ms2 + NORM_EPS) * g2_ref[...]).astype(h_ref.dtype)
    else:
        o_ref, = rest
    o_ref[...] = y


def norm_residual(z, g, x, g_next=None, tr=256):
    m, d = z.shape
    tr = min(tr, m)
    row = pl.BlockSpec((tr, d), lambda i: (i, 0))
    vec = pl.BlockSpec((1, d), lambda i: (0, 0))
    with_next = g_next is not None
    args = (z, g.reshape(1, d), x) + ((g_next.reshape(1, d),) if with_next else ())
    return pl.pallas_call(
        functools.partial(_norm_res_kernel, with_next=with_next),
        grid=(m // tr,),
        in_specs=[row, vec, row] + ([vec] if with_next else []),
        out_specs=[row, row] if with_next else row,
        out_shape=([jax.ShapeDtypeStruct((m, d), F32), jax.ShapeDtypeStruct((m, d), BF16)] if with_next
                   else jax.ShapeDtypeStruct((m, d), F32)),
        compiler_params=_cparams(("parallel",)),
        name="norm_residual",
    )(*args)


def _merge_kernel(orw_ref, oatt_ref, omem_ref, wr_ref, wa_ref, wm_ref, grw_ref, gatt_ref, gmem_ref, o_ref):
    def branch(o_ref_, w_ref_, g_ref_):
        y = jnp.dot(o_ref_[...], w_ref_[...], preferred_element_type=F32)
        return jax.nn.sigmoid(g_ref_[...]) * y

    m = branch(orw_ref, wr_ref, grw_ref) + branch(oatt_ref, wa_ref, gatt_ref) + branch(omem_ref, wm_ref, gmem_ref)
    o_ref[...] = m.astype(o_ref.dtype)


def gated_merge(o_rw, o_att, o_mem, w_r, w_a, w_m, p_b, gate_col0, *, tm=1024, tn=512):
    m = o_rw.shape[0]
    tm = min(tm, m)
    gb = gate_col0 // tn
    nb = D_MODEL // tn
    assert gate_col0 % tn == 0

    def gate_spec(b):
        return pl.BlockSpec((tm, tn), lambda i, j: (i, gb + b * nb + j))

    return pl.pallas_call(
        _merge_kernel,
        grid=(m // tm, D_MODEL // tn),
        in_specs=[pl.BlockSpec((tm, D_RWKV), lambda i, j: (i, 0)),
                  pl.BlockSpec((tm, D_ATT), lambda i, j: (i, 0)),
                  pl.BlockSpec((tm, D_MEM), lambda i, j: (i, 0)),
                  pl.BlockSpec((D_RWKV, tn), lambda i, j: (0, j)),
                  pl.BlockSpec((D_ATT, tn), lambda i, j: (0, j)),
                  pl.BlockSpec((D_MEM, tn), lambda i, j: (0, j)),
                  gate_spec(0), gate_spec(1), gate_spec(2)],
        out_specs=pl.BlockSpec((tm, tn), lambda i, j: (i, j)),
        out_shape=jax.ShapeDtypeStruct((m, D_MODEL), BF16),
        compiler_params=_cparams(("parallel", "parallel")),
        name="gated_merge",
    )(o_rw, o_att, o_mem, w_r, w_a, w_m, p_b, p_b, p_b)


RW_QUAD = 4 * RW_HEAD
RW_NQ = D_RWKV // RW_QUAD
RW_ROWS = RW_NQ * RW_HEAD
Y_TILE = 64


LORA_COLS = RW_COLS_PAD - 3 * D_RWKV


def _rwkv_pre_kernel(p_ref, prev_ref, mu_ref, w0_ref, a0_ref, kk_scale_ref, ka_ref, rk_ref,
                     wl_ref, al_ref, gl_ref, ones2_ref,
                     w_o, r_o, k_o, kk_o, b_o, v_o, bonus_o, g_o, *, boundary):
    p = p_ref[...]
    rows = p.shape[0]
    if boundary:
        row_id = lax.broadcasted_iota(jnp.int32, (rows, 1), 0)
        prev = jnp.where(row_id == 0, prev_ref[0:1, :], pltpu.roll(p, 1, 0))
    else:
        prev = prev_ref[...]
    xs = p + (prev - p) * mu_ref[...]
    r = xs[:, 0:D_RWKV]
    k = xs[:, D_RWKV:2 * D_RWKV]
    v = xs[:, 2 * D_RWKV:3 * D_RWKV]
    lo = xs[:, 3 * D_RWKV:]
    mm = lambda x, w_ref_: jnp.dot(x.astype(BF16), w_ref_[...], preferred_element_type=F32)
    w = -jax.nn.softplus(-(w0_ref[...] + mm(jnp.tanh(lo), wl_ref))) - 0.5
    a = jax.nn.sigmoid(a0_ref[...] + mm(lo, al_ref))
    def put(o_ref, q, x):
        o_ref[2 * q] = x[:, :LANES]
        o_ref[2 * q + 1] = x[:, LANES:]

    g_o[...] = mm(jax.nn.sigmoid(lo), gl_ref)
    decay = jnp.exp(-jnp.exp(w))
    k_mod = k * (1.0 + (a - 1.0) * ka_ref[...])
    kk = k * kk_scale_ref[...]
    rk = r * k_mod * rk_ref[...]
    v_o[...] = v
    for q in range(RW_NQ):
        sl = slice(q * RW_QUAD, (q + 1) * RW_QUAD)
        kq = kk[:, sl]
        norm = jnp.maximum(jnp.sqrt(_seg_sum(kq * kq, ones2_ref[...])), 1e-12)
        kq = kq / norm
        put(kk_o, q, kq)
        put(b_o, q, kq * a[:, sl])
        vq = v[:, sl]
        bonus_o[:, sl] = _seg_sum(rk[:, sl], ones2_ref[...]) * vq
        put(w_o, q, decay[:, sl])
        put(r_o, q, r[:, sl])
        put(k_o, q, k_mod[:, sl])


def rwkv_pre(p_a, prev, boundary, T, W, *, tr=128):
    m = p_a.shape[0]
    tr = min(tr, m)
    assert (not boundary or T % tr == 0) and m % tr == 0
    padc = lambda t: jnp.pad(t, (0, RW_COLS_PAD - RW_COLS)).reshape(1, RW_COLS_PAD)
    vec = lambda t: t.reshape(1, D_RWKV)

    def lora_pad(w_up, row0):
        return jnp.zeros((LORA_COLS, D_RWKV), BF16).at[row0:row0 + w_up.shape[0]].set(w_up.astype(BF16))

    ones = _block_diag_ones(BF16)
    ones2 = jnp.concatenate([ones, ones], axis=0)
    row = pl.BlockSpec((tr, RW_COLS_PAD), lambda i: (i, 0))
    prev_spec = pl.BlockSpec((8, RW_COLS_PAD), lambda i: (i, 0)) if boundary else row
    const = lambda shape: pl.BlockSpec(shape, lambda i: (0,) * len(shape))
    out = pl.BlockSpec((tr, D_RWKV), lambda i: (i, 0))
    nlb = D_RWKV // LANES
    out_lb = pl.BlockSpec((nlb, tr, LANES), lambda i: (0, i, 0))
    return pl.pallas_call(
        functools.partial(_rwkv_pre_kernel, boundary=boundary),
        grid=(m // tr,),
        in_specs=[row, prev_spec, const((1, RW_COLS_PAD))] + [const((1, D_RWKV))] * 5
                 + [const((LORA_COLS, D_RWKV))] * 3 + [const((2 * RW_QUAD, RW_QUAD))],
        out_specs=[out_lb] * 5 + [out] * 3,
        out_shape=[jax.ShapeDtypeStruct((nlb, m, LANES), F32)] * 5 + [jax.ShapeDtypeStruct((m, D_RWKV), F32)] * 3,
        compiler_params=_cparams(("parallel",)),
        name="rwkv_pre",
    )(p_a, prev, padc(W['mu_shift']), vec(W['w0']), vec(W['a0']), vec(W['k_k']), vec(W['k_a']), vec(W['r_k']),
      lora_pad(W['w_lora_up'], 0), lora_pad(W['a_lora_up'], DECAY_LORA),
      lora_pad(W['g_lora_up'], DECAY_LORA + AAA_LORA), ones2)


SCAN_BATCH = 4


def _block_diag_ones(dtype):
    r2 = lax.broadcasted_iota(jnp.int32, (RW_QUAD, RW_QUAD), 0) // RW_HEAD
    c2 = lax.broadcasted_iota(jnp.int32, (RW_QUAD, RW_QUAD), 1) // RW_HEAD
    return (r2 == c2).astype(dtype)


def _seg_sum(x, ones2):
    hi = x.astype(BF16)
    lo = (x - hi.astype(F32)).astype(BF16)
    return jnp.dot(jnp.concatenate([hi, lo], axis=1), ones2, preferred_element_type=F32)


def _repeat_row(ref, lb, i, t):
    return ref[lb, i, pl.ds(t, RW_HEAD, stride=0), :]


def _seg_sum_xlane(x):
    first = lax.broadcasted_iota(jnp.int32, (1, LANES), 1) < RW_HEAD
    cols = []
    for c in range(x.shape[1] // LANES):
        xc = x[:, c * LANES:(c + 1) * LANES]
        sa = jnp.sum(jnp.where(first, xc, 0.0), axis=-1, keepdims=True)
        sb = jnp.sum(jnp.where(first, 0.0, xc), axis=-1, keepdims=True)
        cols.append(jnp.where(first, sa, sb))
    return jnp.concatenate(cols, axis=1)


V_HALF = 32


def _scan_kernel(w_ref, r_ref, k_ref, kk_ref, b_ref, v_ref, bonus_ref, g_ref, s0_ref,
                 tsel_ref, ones_ref, ones2_ref, lnw_ref, lnb_ref,
                 o_ref, s_ref, y_ref, nat_ref, vt_ref, vtmp_ref, *, tc, nb):
    c = pl.program_id(1)
    n_half = (tc + V_HALF - 1) // V_HALF

    @pl.when(c == 0)
    def _():
        s_ref[...] = s0_ref[...]

    y_ref[...] = jnp.zeros(y_ref.shape, F32)
    lane_t = lax.broadcasted_iota(jnp.int32, (1, RW_QUAD), 1) % RW_HEAD

    for i in range(nb):
        v = v_ref[i]
        if tc < LANES:
            v = jnp.concatenate([v, jnp.zeros((LANES - tc, D_RWKV), F32)], axis=0)
        vt = v.T
        hi = vt.astype(BF16).astype(F32)
        pieces = (hi, vt - hi)
        for a in range(n_half):
            for q in range(RW_NQ):
                for pc in range(2):
                    for h4 in range(4):
                        src = pieces[pc][(q * 4 + h4) * RW_HEAD:(q * 4 + h4 + 1) * RW_HEAD, a * V_HALF:(a + 1) * V_HALF]
                        dst = (pc * 4 + h4) * V_HALF
                        vtmp_ref[q * RW_HEAD:(q + 1) * RW_HEAD, dst:dst + V_HALF] = src
            vt_ref[a, i * RW_ROWS:(i + 1) * RW_ROWS, :] = vtmp_ref[...].astype(BF16)

    def stack(ref, i, t):
        rep = lambda lb: _repeat_row(ref, lb, i, t)
        return jnp.concatenate(
            [jnp.concatenate([rep(2 * q), rep(2 * q + 1)], axis=1) for q in range(RW_NQ)], axis=0)

    def body(t, carry, a):
        sel = jnp.where(tsel_ref[...] == t - a * V_HALF, 1.0, 0.0).astype(BF16)
        rows = lambda ref: jnp.concatenate([stack(ref, i, t) for i in range(nb)], axis=0)
        s = jnp.concatenate([s_ref[i] for i in range(nb)], axis=0)
        p = s * rows(kk_ref)
        xl = nb * RW_ROWS // 2
        s_kk = jnp.concatenate([_seg_sum_xlane(p[:xl]), _seg_sum(p[xl:], ones2_ref[...])], axis=0)
        v_col = jnp.dot(vt_ref[a], sel, preferred_element_type=F32)
        s = s * rows(w_ref) - s_kk * rows(b_ref) + v_col * rows(k_ref)
        for i in range(nb):
            s_ref[i] = s[i * RW_ROWS:(i + 1) * RW_ROWS]
        y_b = jnp.dot((s * rows(r_ref)).astype(BF16), ones_ref[...], preferred_element_type=F32)
        y_ref[...] = jnp.where(lane_t == t, y_b, y_ref[...])
        return carry

    for a in range(n_half):
        lax.fori_loop(a * V_HALF, min(tc, (a + 1) * V_HALF), functools.partial(body, a=a), 0, unroll=2)

    for i in range(nb):
        for q in range(RW_NQ):
            rq = slice(i * RW_ROWS + q * RW_HEAD, i * RW_ROWS + (q + 1) * RW_HEAD)
            yq = y_ref[rq, :]
            mu = jnp.mean(yq, axis=0, keepdims=True)
            d = yq - mu
            var = jnp.mean(d * d, axis=0, keepdims=True)
            y_ref[rq, :] = d * lax.rsqrt(var + GN_EPS)
        yt = y_ref[i * RW_ROWS:(i + 1) * RW_ROWS, :].T
        for q in range(RW_NQ):
            for h4 in range(4):
                ch = (q * 4 + h4) * RW_HEAD
                nat_ref[:, ch:ch + RW_HEAD] = yt[h4 * RW_HEAD:h4 * RW_HEAD + tc, q * RW_HEAD:(q + 1) * RW_HEAD]
        o = (nat_ref[...] * lnw_ref[...] + lnb_ref[...] + bonus_ref[i]) * g_ref[i]
        o_ref[i] = o.astype(o_ref.dtype)


def rwkv_scan(w, r, k, kk, b, v, bonus, g, s0, ln_w, ln_b):
    B, T, _ = bonus.shape
    tc = min(Y_TILE, T)
    nc = T // tc
    nb = SCAN_BATCH
    assert T % tc == 0 and B % nb == 0
    s0q = s0.reshape(B, RW_NQ, 4, RW_HEAD, RW_HEAD).transpose(0, 1, 3, 2, 4).reshape(B, RW_ROWS, RW_QUAD)
    rr = lax.broadcasted_iota(jnp.int32, (RW_QUAD, RW_QUAD), 0)
    cc = lax.broadcasted_iota(jnp.int32, (RW_QUAD, RW_QUAD), 1)
    tsel = jnp.where((rr // V_HALF) % 4 == cc // RW_HEAD, rr % V_HALF, -1)
    ones = _block_diag_ones(BF16)
    ones2 = jnp.concatenate([ones, ones], axis=0)
    xspec = pl.BlockSpec((nb, tc, D_RWKV), lambda bi, ci: (bi, ci, 0))
    sspec = pl.BlockSpec((nb, RW_ROWS, RW_QUAD), lambda bi, ci: (bi, 0, 0))
    const = lambda shape: pl.BlockSpec(shape, lambda bi, ci: (0,) * len(shape))
    o, s_fin = pl.pallas_call(
        functools.partial(_scan_kernel, tc=tc, nb=nb),
        grid=(B // nb, nc),
        in_specs=[pl.BlockSpec((D_RWKV // LANES, nb, tc, LANES), lambda bi, ci: (0, bi, ci, 0))] * 5
                 + [xspec] * 3 + [sspec, const((RW_QUAD, RW_QUAD)),
                                const((RW_QUAD, RW_QUAD)), const((2 * RW_QUAD, RW_QUAD)),
                                const((1, D_RWKV)), const((1, D_RWKV))],
        out_specs=[xspec, sspec],
        out_shape=[jax.ShapeDtypeStruct((B, T, D_RWKV), BF16),
                   jax.ShapeDtypeStruct((B, RW_ROWS, RW_QUAD), F32)],
        scratch_shapes=[pltpu.VMEM((nb * RW_ROWS, RW_QUAD), F32), pltpu.VMEM((tc, D_RWKV), F32),
                        pltpu.VMEM(((tc + V_HALF - 1) // V_HALF, nb * RW_ROWS, RW_QUAD), BF16),
                        pltpu.VMEM((RW_ROWS, RW_QUAD), F32)],
        compiler_params=_cparams(("parallel", "arbitrary")),
        name="rwkv_scan",
    )(w, r, k, kk, b, v, bonus, g, s0q, tsel, ones, ones2,
      ln_w.reshape(1, D_RWKV), ln_b.reshape(1, D_RWKV))
    s_fin = s_fin.reshape(B, RW_NQ, RW_HEAD, 4, RW_HEAD).transpose(0, 1, 3, 2, 4).reshape(B, RW_HEADS, RW_HEAD, RW_HEAD)
    return o, s_fin


ATT_SCALE = ATT_QK ** -0.5


def _lam_from(lamp_ref):
    lp = lamp_ref[...]
    return (jnp.exp(jnp.sum(lp[0:1] * lp[1:2])) - jnp.exp(jnp.sum(lp[2:3] * lp[3:4])) + LAM_INIT)


def _online_softmax_step(s, v_bf, m, l, a):
    m_new = jnp.maximum(m, jnp.max(s, axis=-1, keepdims=True))
    alpha = jnp.exp(m - m_new)
    p = jnp.exp(s - m_new)
    l = alpha * l + jnp.sum(p, axis=-1, keepdims=True)
    a = alpha * a + jnp.dot(p.astype(BF16), v_bf, preferred_element_type=F32)
    return m_new, l, a


def _subln(o, g):
    ms = jnp.mean(o * o, axis=-1, keepdims=True)
    return o * lax.rsqrt(ms + NORM_EPS) * g * (1.0 - LAM_INIT)


def _diff_prompt_kernel(slopes_ref, q_ref, k_ref, v_ref, lamp_ref, g_ref, o_ref, *, tq, tk):
    h = pl.program_id(1)
    qi = pl.program_id(2)
    slope = slopes_ref[h]
    lam = _lam_from(lamp_ref)
    q = q_ref[...]
    q1 = q[:, :ATT_QK].astype(BF16)
    q2 = q[:, ATT_QK:].astype(BF16)
    rel = (lax.broadcasted_iota(jnp.int32, (tq, tk), 0) - lax.broadcasted_iota(jnp.int32, (tq, tk), 1))
    base = -slope * rel.astype(F32)

    def kv_step(kb, carry, diagonal=False):
        m1, l1, a1, m2, l2, a2 = carry
        k0 = pl.multiple_of(kb * tk, tk)
        k = k_ref[pl.ds(k0, tk), :]
        v = v_ref[pl.ds(k0, tk), :].astype(BF16)
        bias = base - slope * (qi * tq - k0).astype(F32)
        if diagonal:
            bias = jnp.where(rel >= 0, bias, NEG_INF)
        s1 = lax.dot_general(q1, k[:, :ATT_QK].astype(BF16), DOT_NT, preferred_element_type=F32) * ATT_SCALE + bias
        s2 = lax.dot_general(q2, k[:, ATT_QK:].astype(BF16), DOT_NT, preferred_element_type=F32) * ATT_SCALE + bias
        m1, l1, a1 = _online_softmax_step(s1, v, m1, l1, a1)
        m2, l2, a2 = _online_softmax_step(s2, v, m2, l2, a2)
        return m1, l1, a1, m2, l2, a2

    col = lambda val: jnp.full((tq, 1), val, F32)
    acc0 = jnp.zeros((tq, ATT_V), F32)
    carry = lax.fori_loop(0, qi, kv_step, (col(NEG_INF), col(0.0), acc0, col(NEG_INF), col(0.0), acc0))
    m1, l1, a1, m2, l2, a2 = kv_step(qi, carry, diagonal=True)
    o = a1 / l1 - lam * (a2 / l2)
    o_ref[...] = _subln(o, g_ref[...]).astype(o_ref.dtype)


def diff_attn_prompt(q, k, v, B, T, slopes, lamp, g_subln, *, tq=512, tk=512):
    assert tq == tk and T % tq == 0
    nq = T // tq
    return pl.pallas_call(
        functools.partial(_diff_prompt_kernel, tq=tq, tk=tk),
        grid=(B, ATT_HEADS, nq),
        in_specs=[pl.BlockSpec(memory_space=pltpu.SMEM),
                  pl.BlockSpec((tq, ATT_V), lambda b, h, i: (b * nq + i, h)),
                  pl.BlockSpec((T, ATT_V), lambda b, h, i: (b, h)),
                  pl.BlockSpec((T, ATT_V), lambda b, h, i: (b, h)),
                  pl.BlockSpec((4, ATT_QK), lambda b, h, i: (0, 0)),
                  pl.BlockSpec((1, ATT_V), lambda b, h, i: (0, 0))],
        out_specs=pl.BlockSpec((tq, ATT_V), lambda b, h, i: (b * nq + i, h)),
        out_shape=jax.ShapeDtypeStruct((B * T, D_ATT), BF16),
        compiler_params=_cparams(("parallel", "parallel", "arbitrary")),
        name="diff_attn_prompt",
    )(slopes, q, k, v, lamp, g_subln.reshape(1, ATT_V))


PAGES_PER_STEP = 8
QROWS = 16


PAGE_ROWS = PAGE_SIZE * ATT_HEADS
DOT_TN = (((0,), (0,)), ((), ()))


def _diff_sample_kernel(pt_ref, slopes_ref, q_ref, kn_ref, vn_ref, lamp_ref, g_ref, *rest, ds, past):
    k_refs = rest[:PAGES_PER_STEP]
    v_refs = rest[PAGES_PER_STEP:2 * PAGES_PER_STEP]
    o_ref = rest[2 * PAGES_PER_STEP]
    qw_ref, bias_ref, m_ref, l_ref, a_ref = rest[2 * PAGES_PER_STEP + 1:]
    j = pl.program_id(1)
    cph = 2 * ds
    col = lax.broadcasted_iota(jnp.int32, (1, LANES), 1)
    col_head = col // cph
    col_q = col % ds
    slope_col = jnp.zeros((1, LANES), F32)
    for h in range(ATT_HEADS):
        slope_col = jnp.where(col_head == h, slopes_ref[h], slope_col)
    row_key = lax.broadcasted_iota(jnp.int32, (PAGE_ROWS, 1), 0) // ATT_HEADS
    row_head = lax.broadcasted_iota(jnp.int32, (PAGE_ROWS, 1), 0) % ATT_HEADS

    @pl.when(j == 0)
    def _():
        lane = lax.broadcasted_iota(jnp.int32, (ds, ATT_V), 1)
        rows = []
        for h in range(ATT_HEADS):
            qh = q_ref[0, :, h * ATT_V:(h + 1) * ATT_V]
            rows += [jnp.where(lane < ATT_QK, qh, 0.0), jnp.where(lane >= ATT_QK, qh, 0.0)]
        rows.append(jnp.zeros((LANES - ATT_HEADS * cph, ATT_V), F32))
        qw_ref[...] = jnp.concatenate(rows, axis=0).T.astype(BF16)
        bias_ref[...] = jnp.where(row_head == col_head, slope_col * row_key.astype(F32), NEG_INF)
        m_ref[...] = jnp.full(m_ref.shape, NEG_INF, F32)
        l_ref[...] = jnp.zeros(l_ref.shape, F32)
        a_ref[...] = jnp.zeros(a_ref.shape, F32)

    eye = (lax.broadcasted_iota(jnp.int32, (LANES, LANES), 0) == lax.broadcasted_iota(jnp.int32, (LANES, LANES), 1))
    to_col = lambda r: jnp.sum(jnp.where(eye, r, 0.0), axis=1, keepdims=True)

    def attend(k_list, v_list, key0, extra):
        s_pages = []
        for n, k_ref in enumerate(k_list):
            k2 = k_ref[0].reshape(PAGE_ROWS, ATT_V).astype(BF16)
            s = jnp.dot(k2, qw_ref[...], preferred_element_type=F32) * ATT_SCALE + bias_ref[...]
            s_pages.append(s + slope_col * (key0 + n * PAGE_SIZE - past - col_q).astype(F32))
        s = jnp.concatenate(s_pages, axis=0)
        if extra is not None:
            s = jnp.where(extra, s, NEG_INF)
        v2 = jnp.concatenate([v_ref[0].reshape(PAGE_ROWS, ATT_V).astype(BF16) for v_ref in v_list], axis=0)
        m_old = m_ref[...]
        m_new = jnp.maximum(m_old, jnp.max(s, axis=0, keepdims=True))
        alpha = jnp.exp(m_old - m_new)
        p = jnp.exp(s - m_new)
        l_ref[...] = alpha * l_ref[...] + jnp.sum(p, axis=0, keepdims=True)
        m_ref[...] = m_new
        pv = lax.dot_general(p.astype(BF16), v2, DOT_TN, preferred_element_type=F32)
        a_ref[...] = a_ref[...] * to_col(alpha) + pv

    attend(k_refs, v_refs, j * PAGES_PER_STEP * PAGE_SIZE, None)

    @pl.when(j == pl.num_programs(1) - 1)
    def _():
        attend([kn_ref], [vn_ref], past, (row_key < ds) & (row_key <= col_q))
        lam = _lam_from(lamp_ref)
        o_all = a_ref[...] / to_col(l_ref[...])
        for h in range(ATT_HEADS):
            o = o_all[h * cph:h * cph + ds] - lam * o_all[h * cph + ds:(h + 1) * cph]
            o_ref[0, :, h * ATT_V:(h + 1) * ATT_V] = _subln(o, g_ref[...]).astype(o_ref.dtype)


def diff_attn_sample(q, k_new, v_new, cache_k, cache_v, page_table, slopes, lamp, g_subln):
    B, ds, _ = q.shape
    n_pages = page_table.shape[1]
    past = n_pages * PAGE_SIZE
    assert n_pages % PAGES_PER_STEP == 0 and ATT_HEADS * 2 * ds <= LANES
    pad = lambda t: jnp.pad(t, ((0, 0), (0, PAGE_SIZE - ds), (0, 0), (0, 0)))
    page_blk = (1, PAGE_SIZE, ATT_HEADS, ATT_V)

    def page_spec(i):
        return pl.BlockSpec(page_blk, lambda b, j, pt: (pt[b, j * PAGES_PER_STEP + i], 0, 0, 0))

    fixed = lambda shape: pl.BlockSpec(shape, lambda b, j, pt: (0,) * len(shape))
    grid_spec = pltpu.PrefetchScalarGridSpec(
        num_scalar_prefetch=1,
        grid=(B, n_pages // PAGES_PER_STEP),
        in_specs=[pl.BlockSpec(memory_space=pltpu.SMEM),
                  pl.BlockSpec((1, ds, D_ATT), lambda b, j, pt: (b, 0, 0)),
                  pl.BlockSpec(page_blk, lambda b, j, pt: (b, 0, 0, 0)),
                  pl.BlockSpec(page_blk, lambda b, j, pt: (b, 0, 0, 0)),
                  fixed((4, ATT_QK)), fixed((1, ATT_V))]
                 + [page_spec(i) for i in range(PAGES_PER_STEP)] * 2,
        out_specs=pl.BlockSpec((1, ds, D_ATT), lambda b, j, pt: (b, 0, 0)),
        scratch_shapes=[pltpu.VMEM((ATT_V, LANES), BF16),
                        pltpu.VMEM((PAGE_ROWS, LANES), F32),
                        pltpu.VMEM((1, LANES), F32),
                        pltpu.VMEM((1, LANES), F32),
                        pltpu.VMEM((LANES, ATT_V), F32)])
    return pl.pallas_call(
        functools.partial(_diff_sample_kernel, ds=ds, past=past),
        grid_spec=grid_spec,
        out_shape=jax.ShapeDtypeStruct((B, ds, D_ATT), BF16),
        compiler_params=_cparams(("parallel", "arbitrary")),
        name="diff_attn_sample",
    )(page_table, slopes, q, pad(k_new), pad(v_new), lamp, g_subln.reshape(1, ATT_V),
      *([cache_k] * PAGES_PER_STEP), *([cache_v] * PAGES_PER_STEP))


def _mem_attn_kernel(q_ref, k_ref, v_ref, o_ref, *, paged_heads, rows):
    for h in range(MEM_HEADS):
        sl = slice(h * MEM_HEAD, (h + 1) * MEM_HEAD)
        q = q_ref[:, sl] if q_ref.ndim == 2 else q_ref[0, :, sl]
        if q.shape[0] < rows:
            q = jnp.concatenate([q, jnp.zeros((rows - q.shape[0], MEM_HEAD), F32)], axis=0)
        k = k_ref[0, :, h, :] if paged_heads else k_ref[:, sl]
        v = v_ref[0, :, h, :] if paged_heads else v_ref[:, sl]
        s = lax.dot_general(q.astype(BF16), k.astype(BF16), DOT_NT, preferred_element_type=F32) * (MEM_HEAD ** -0.5)
        p = jnp.exp(s - jnp.max(s, axis=-1, keepdims=True))
        o = jnp.dot(p.astype(BF16), v.astype(BF16), preferred_element_type=F32) / jnp.sum(p, axis=-1, keepdims=True)
        if q_ref.ndim == 2:
            o_ref[:, sl] = o.astype(o_ref.dtype)
        else:
            o_ref[0, :, sl] = o[:q_ref.shape[1]].astype(o_ref.dtype)


def mem_attn_prompt(p_c, kv, B, T, MT, *, tq=512):
    tq = min(tq, T)
    nq = T // tq
    return pl.pallas_call(
        functools.partial(_mem_attn_kernel, paged_heads=False, rows=tq),
        grid=(B, nq),
        in_specs=[pl.BlockSpec((tq, D_MEM), lambda b, i: (b * nq + i, 0)),
                  pl.BlockSpec((MT, D_MEM), lambda b, i: (b, 0)),
                  pl.BlockSpec((MT, D_MEM), lambda b, i: (b, 1))],
        out_specs=pl.BlockSpec((tq, D_MEM), lambda b, i: (b * nq + i, 0)),
        out_shape=jax.ShapeDtypeStruct((B * T, D_MEM), BF16),
        compiler_params=_cparams(("parallel", "parallel")),
        name="mem_attn_prompt",
    )(p_c, kv, kv)


def mem_attn_sample(qm, mem_k, mem_v):
    B, ds, _ = qm.shape
    MT = mem_k.shape[1]
    kv_spec = pl.BlockSpec((1, MT, MEM_HEADS, MEM_HEAD), lambda b: (b, 0, 0, 0))
    return pl.pallas_call(
        functools.partial(_mem_attn_kernel, paged_heads=True, rows=QROWS),
        grid=(B,),
        in_specs=[pl.BlockSpec((1, ds, D_MEM), lambda b: (b, 0, 0)), kv_spec, kv_spec],
        out_specs=pl.BlockSpec((1, ds, D_MEM), lambda b: (b, 0, 0)),
        out_shape=jax.ShapeDtypeStruct((B, ds, D_MEM), BF16),
        compiler_params=_cparams(("parallel",)),
        name="mem_attn_sample",
    )(qm, mem_k, mem_v)


PRE_ROWS = 128


def _rwkv7_branch(p_a, B, T, shift_prev, s0, W):
    M = B * T
    sp = jnp.pad(shift_prev, ((0, 0), (0, RW_COLS_PAD - RW_COLS)))
    if T % PRE_ROWS == 0:
        first = jnp.arange(M // PRE_ROWS) * PRE_ROWS
        rows = jnp.where((first % T == 0)[:, None], sp[first // T], p_a[jnp.maximum(first - 1, 0)])
        prev = jnp.zeros((M // PRE_ROWS, 8, RW_COLS_PAD), F32).at[:, 0].set(rows).reshape(-1, RW_COLS_PAD)
        boundary = True
    else:
        p3 = p_a.reshape(B, T, RW_COLS_PAD)
        prev = jnp.concatenate([sp[:, None], p3[:, :-1]], axis=1).reshape(M, RW_COLS_PAD)
        boundary = False
    outs = rwkv_pre(p_a, prev, boundary, T, W, tr=PRE_ROWS)
    per_step = [t.reshape(D_RWKV // LANES, B, T, LANES) for t in outs[:5]]
    v, bonus, g = (t.reshape(B, T, D_RWKV) for t in outs[5:])
    o, s_fin = rwkv_scan(*per_step, v, bonus, g, s0, W['ln_x_w'], W['ln_x_b'])
    shift_last = p_a.reshape(B, T, RW_COLS_PAD)[:, -1, :RW_COLS]
    return o.reshape(M, D_RWKV), s_fin, shift_last


def _layer(x, shift_prev, s0, attend, mem_attend, W):
    B, T, _ = x.shape
    M = B * T
    x2 = x.reshape(M, D_MODEL)
    h = rmsnorm_bf16(x2, W['g_pre_mix'])
    wt = W['w_in_t']
    p_a = matmul_nt(h, wt, n_cols=RW_COLS_PAD)
    q = matmul_nt(h, wt, row0=RW_COLS, n_cols=D_ATT)
    k = matmul_nt(h, wt, row0=RW_COLS + D_ATT, n_cols=D_ATT)
    v = matmul_nt(h, wt, row0=RW_COLS + 2 * D_ATT, n_cols=D_ATT)
    p_c = matmul_nt(h, wt, row0=RW_COLS + 3 * D_ATT, n_cols=D_MEM + 3 * D_MODEL)

    o_rw, s_fin, shift_last = _rwkv7_branch(p_a, B, T, shift_prev, s0, W)
    o_att = attend(q, k, v)
    o_mem = mem_attend(p_c)

    m = gated_merge(o_rw, o_att, o_mem,
                    W['w_br_rwkv'], W['w_br_attn'], W['w_br_mem'], p_c, D_MEM)
    k = k.reshape(B, T, ATT_HEADS, 2 * ATT_QK)
    v = v.reshape(B, T, ATT_HEADS, ATT_V)
    z = matmul(m, W['w_out'])
    x1, h2 = norm_residual(z, W['g_post_mix'], x2, W['g_pre_mlp'])
    u = matmul(h2, W['w_up'], out_dtype=BF16, relu2=True)
    z2 = matmul(u, W['w_down'], tm=2048, tk=2048)
    y = norm_residual(z2, W['g_post_mlp'], x1)
    return y.reshape(B, T, D_MODEL), k, v, s_fin, shift_last


def kernel(x_prompt, x_sample, cache_k, cache_v, page_table, state_wkv, state_shift, cache_mem_k, cache_mem_v, mem_prompt, g_pre_mix, w_in, mu_shift, w0, w_lora_up, a0, a_lora_up, g_lora_up, k_k, k_a, r_k, ln_x_w, ln_x_b, lambda_q1, lambda_k1, lambda_q2, lambda_k2, g_subln, g_mem, w_mem_kv, w_br_rwkv, w_br_attn, w_br_mem, w_out, g_post_mix, g_pre_mlp, w_up, w_down, g_post_mlp):
    W = dict(g_pre_mix=g_pre_mix, w_in_t=w_in.T,
             mu_shift=mu_shift, w0=w0, w_lora_up=w_lora_up,
             a0=a0, a_lora_up=a_lora_up, g_lora_up=g_lora_up, k_k=k_k, k_a=k_a, r_k=r_k,
             ln_x_w=ln_x_w, ln_x_b=ln_x_b, lambda_q1=lambda_q1, lambda_k1=lambda_k1,
             lambda_q2=lambda_q2, lambda_k2=lambda_k2, g_subln=g_subln, w_br_rwkv=w_br_rwkv.astype(BF16),
             w_br_attn=w_br_attn.astype(BF16), w_br_mem=w_br_mem.astype(BF16), w_out=w_out,
             g_post_mix=g_post_mix,
             g_pre_mlp=g_pre_mlp, w_up=w_up, w_down=w_down, g_post_mlp=g_post_mlp)
    slopes = jnp.asarray([2.0 ** (-8.0 * (h + 1) / ATT_HEADS) for h in range(ATT_HEADS)], dtype=F32)
    lamp = jnp.stack([lambda_q1, lambda_k1, lambda_q2, lambda_k2])
    B, T, _ = x_prompt.shape
    BS, DS, _ = x_sample.shape
    MT = mem_prompt.shape[1]

    hm = rmsnorm_bf16(mem_prompt.reshape(B * MT, D_MODEL), g_mem)
    kv = matmul(hm, w_mem_kv)
    mem_k_prompt = kv[:, :D_MEM].reshape(B, MT, MEM_HEADS, MEM_HEAD)
    mem_v_prompt = kv[:, D_MEM:].reshape(B, MT, MEM_HEADS, MEM_HEAD)

    y_prompt, k_prompt, v_prompt, wkv_prompt, shift_prompt = _layer(
        x_prompt, jnp.zeros((B, RW_COLS), F32), jnp.zeros((B, RW_HEADS, RW_HEAD, RW_HEAD), F32),
        lambda q, k, v: diff_attn_prompt(q, k, v, B, T, slopes, lamp, g_subln),
        lambda p_c: mem_attn_prompt(p_c, kv, B, T, MT), W)

    def attend_sample(q, k, v):
        page = lambda t: t.reshape(BS, DS, ATT_HEADS, ATT_V)
        o = diff_attn_sample(q.reshape(BS, DS, D_ATT), page(k), page(v), cache_k, cache_v, page_table,
                             slopes, lamp, g_subln)
        return o.reshape(BS * DS, D_ATT)

    def mem_attend_sample(p_c):
        qm = p_c[:, :D_MEM].reshape(BS, DS, D_MEM)
        return mem_attn_sample(qm, cache_mem_k, cache_mem_v).reshape(BS * DS, D_MEM)

    y_sample, k_sample, v_sample, wkv_sample, shift_sample = _layer(
        x_sample, state_shift, state_wkv, attend_sample, mem_attend_sample, W)
    return (y_prompt, y_sample, k_prompt, v_prompt, k_sample, v_sample,
            wkv_prompt, wkv_sample, shift_prompt, shift_sample, mem_k_prompt, mem_v_prompt)
```

```python
import functools
import math

import jax
import jax.numpy as jnp
from jax import lax
from jax.experimental import pallas as pl
from jax.experimental.pallas import tpu as pltpu

F32 = jnp.float32
BF16 = jnp.bfloat16

D_MODEL = 4096
PAGE_SIZE = 128
D_RWKV = 2048
RW_HEAD = 64
RW_HEADS = 32
DECAY_LORA = 96
AAA_LORA = 96
GATE_LORA = 256
RW_COLS = 3 * D_RWKV + DECAY_LORA + AAA_LORA + GATE_LORA
RW_COLS_PAD = 6656
GN_EPS = 64e-5
D_ATT = 2048
ATT_HEADS = 8
ATT_QK = 128
ATT_V = 256
MEM_HEADS = 4
MEM_HEAD = 128
D_MEM = 512
NORM_EPS = 1e-6
NEG_INF = -1e30
LAM_INIT = 0.8 - 0.6 * math.exp(-0.3 * 0)

VMEM_LIMIT = 56 * 1024 * 1024
LANES = 128


def _cparams(sem):
    return pltpu.CompilerParams(dimension_semantics=sem, vmem_limit_bytes=VMEM_LIMIT)


def _rmsnorm_kernel(x_ref, g_ref, o_ref):
    x = x_ref[...]
    ms = jnp.mean(x * x, axis=-1, keepdims=True)
    o_ref[...] = (x * lax.rsqrt(ms + NORM_EPS) * g_ref[...]).astype(o_ref.dtype)


def rmsnorm_bf16(x, g, tr=256):
    m, d = x.shape
    tr = min(tr, m)
    return pl.pallas_call(
        _rmsnorm_kernel,
        grid=(m // tr,),
        in_specs=[pl.BlockSpec((tr, d), lambda i: (i, 0)),
                  pl.BlockSpec((1, d), lambda i: (0, 0))],
        out_specs=pl.BlockSpec((tr, d), lambda i: (i, 0)),
        out_shape=jax.ShapeDtypeStruct((m, d), BF16),
        compiler_params=_cparams(("parallel",)),
        name="rmsnorm_bf16",
    )(x, g.reshape(1, d))


DOT_NT = (((1,), (1,)), ((), ()))


def _mm_nt_kernel(a_ref, bt_ref, o_ref):
    o_ref[...] = lax.dot_general(a_ref[...], bt_ref[...].astype(BF16), DOT_NT,
                                 preferred_element_type=F32).astype(o_ref.dtype)


def matmul_nt(a, bt, *, row0=0, n_cols, tm=2048, tn=512):
    m, kd = a.shape
    tm = min(tm, m)
    assert m % tm == 0 and n_cols % tn == 0 and row0 % 8 == 0 and bt.shape[1] == kd
    return pl.pallas_call(
        _mm_nt_kernel,
        grid=(m // tm, n_cols // tn),
        in_specs=[pl.BlockSpec((tm, kd), lambda i, j: (i, 0), pipeline_mode=pl.Buffered(1)),
                  pl.BlockSpec((pl.Element(tn), pl.Element(kd)),
                               lambda i, j: ((row0 // 8 + j * (tn // 8)) * 8, 0))],
        out_specs=pl.BlockSpec((tm, tn), lambda i, j: (i, j)),
        out_shape=jax.ShapeDtypeStruct((m, n_cols), F32),
        compiler_params=_cparams(("parallel", "parallel")),
        name="matmul_nt",
    )(a, bt)


def _mm_kernel(a_ref, b_ref, o_ref, *scratch, relu2, nk):
    def finish(r):
        if relu2:
            r = jnp.square(jnp.maximum(r, 0.0))
        o_ref[...] = r.astype(o_ref.dtype)

    part = jnp.dot(a_ref[...], b_ref[...].astype(BF16), preferred_element_type=F32)
    if nk == 1:
        finish(part)
        return
    acc_ref, = scratch
    k = pl.program_id(2)

    @pl.when(k == 0)
    def _():
        acc_ref[...] = part

    @pl.when(k > 0)
    def _():
        acc_ref[...] += part

    @pl.when(k == nk - 1)
    def _():
        finish(acc_ref[...])


def matmul(a, b, *, tm=2048, tn=512, tk=4096, out_dtype=F32, relu2=False):
    m, kd = a.shape
    n = b.shape[1]
    tm = min(tm, m)
    tk = min(tk, kd)
    assert m % tm == 0 and n % tn == 0 and kd % tk == 0
    nk = kd // tk
    a_mode = dict(pipeline_mode=pl.Buffered(1)) if nk == 1 else {}
    return pl.pallas_call(
        functools.partial(_mm_kernel, relu2=relu2, nk=nk),
        grid=(m // tm, n // tn, nk),
        in_specs=[pl.BlockSpec((tm, tk), lambda i, j, k: (i, k), **a_mode),
                  pl.BlockSpec((tk, tn), lambda i, j, k: (k, j))],
        out_specs=pl.BlockSpec((tm, tn), lambda i, j, k: (i, j)),
        out_shape=jax.ShapeDtypeStruct((m, n), out_dtype),
        scratch_shapes=[pltpu.VMEM((tm, tn), F32)] if nk > 1 else [],
        compiler_params=_cparams(("parallel", "parallel", "arbitrary")),
        name="matmul",
    )(a, b)


def _norm_res_kernel(z_ref, g_ref, x_ref, *rest, with_next):
    z = z_ref[...]
    ms = jnp.mean(z * z, axis=-1, keepdims=True)
    y = x_ref[...] + z * lax.rsqrt(ms + NORM_EPS) * g_ref[...]
    if with_next:
        g2_ref, o_ref, h_ref = rest
        ms2 = jnp.mean(y * y, axis=-1, keepdims=True)
        h_ref[...] = (y * lax.rsqrt(ms2 + NORM_EPS) * g2_ref[...]).astype(h_ref.dtype)
    else:
        o_ref, = rest
    o_ref[...] = y


def norm_residual(z, g, x, g_next=None, tr=256):
    m, d = z.shape
    tr = min(tr, m)
    row = pl.BlockSpec((tr, d), lambda i: (i, 0))
    vec = pl.BlockSpec((1, d), lambda i: (0, 0))
    with_next = g_next is not None
    args = (z, g.reshape(1, d), x) + ((g_next.reshape(1, d),) if with_next else ())
    return pl.pallas_call(
        functools.partial(_norm_res_kernel, with_next=with_next),
        grid=(m // tr,),
        in_specs=[row, vec, row] + ([vec] if with_next else []),
        out_specs=[row, row] if with_next else row,
        out_shape=([jax.ShapeDtypeStruct((m, d), F32), jax.ShapeDtypeStruct((m, d), BF16)] if with_next
                   else jax.ShapeDtypeStruct((m, d), F32)),
        compiler_params=_cparams(("parallel",)),
        name="norm_residual",
    )(*args)


def _merge_kernel(orw_ref, oatt_ref, omem_ref, wr_ref, wa_ref, wm_ref, grw_ref, gatt_ref, gmem_ref, o_ref):
    def branch(o_ref_, w_ref_, g_ref_):
        y = jnp.dot(o_ref_[...], w_ref_[...], preferred_element_type=F32)
        return jax.nn.sigmoid(g_ref_[...]) * y

    m = branch(orw_ref, wr_ref, grw_ref) + branch(oatt_ref, wa_ref, gatt_ref) + branch(omem_ref, wm_ref, gmem_ref)
    o_ref[...] = m.astype(o_ref.dtype)


def gated_merge(o_rw, o_att, o_mem, w_r, w_a, w_m, p_b, gate_col0, *, tm=1024, tn=512):
    m = o_rw.shape[0]
    tm = min(tm, m)
    gb = gate_col0 // tn
    nb = D_MODEL // tn
    assert gate_col0 % tn == 0

    def gate_spec(b):
        return pl.BlockSpec((tm, tn), lambda i, j: (i, gb + b * nb + j))

    return pl.pallas_call(
        _merge_kernel,
        grid=(m // tm, D_MODEL // tn),
        in_specs=[pl.BlockSpec((tm, D_RWKV), lambda i, j: (i, 0)),
                  pl.BlockSpec((tm, D_ATT), lambda i, j: (i, 0)),
                  pl.BlockSpec((tm, D_MEM), lambda i, j: (i, 0)),
                  pl.BlockSpec((D_RWKV, tn), lambda i, j: (0, j)),
                  pl.BlockSpec((D_ATT, tn), lambda i, j: (0, j)),
                  pl.BlockSpec((D_MEM, tn), lambda i, j: (0, j)),
                  gate_spec(0), gate_spec(1), gate_spec(2)],
        out_specs=pl.BlockSpec((tm, tn), lambda i, j: (i, j)),
        out_shape=jax.ShapeDtypeStruct((m, D_MODEL), BF16),
        compiler_params=_cparams(("parallel", "parallel")),
        name="gated_merge",
    )(o_rw, o_att, o_mem, w_r, w_a, w_m, p_b, p_b, p_b)


RW_QUAD = 4 * RW_HEAD
RW_NQ = D_RWKV // RW_QUAD
RW_ROWS = RW_NQ * RW_HEAD
Y_TILE = 64


LORA_COLS = RW_COLS_PAD - 3 * D_RWKV


def _rwkv_pre_kernel(p_ref, prev_ref, mu_ref, w0_ref, a0_ref, kk_scale_ref, ka_ref, rk_ref,
                     wl_ref, al_ref, gl_ref, ones2_ref,
                     w_o, r_o, k_o, kk_o, b_o, v_o, bonus_o, g_o, *, boundary):
    p = p_ref[...]
    rows = p.shape[0]
    if boundary:
        row_id = lax.broadcasted_iota(jnp.int32, (rows, 1), 0)
        prev = jnp.where(row_id == 0, prev_ref[0:1, :], pltpu.roll(p, 1, 0))
    else:
        prev = prev_ref[...]
    xs = p + (prev - p) * mu_ref[...]
    r = xs[:, 0:D_RWKV]
    k = xs[:, D_RWKV:2 * D_RWKV]
    v = xs[:, 2 * D_RWKV:3 * D_RWKV]
    lo = xs[:, 3 * D_RWKV:]
    mm = lambda x, w_ref_: jnp.dot(x.astype(BF16), w_ref_[...], preferred_element_type=F32)
    w = -jax.nn.softplus(-(w0_ref[...] + mm(jnp.tanh(lo), wl_ref))) - 0.5
    a = jax.nn.sigmoid(a0_ref[...] + mm(lo, al_ref))
    def put(o_ref, q, x):
        o_ref[2 * q] = x[:, :LANES]
        o_ref[2 * q + 1] = x[:, LANES:]

    g_o[...] = mm(jax.nn.sigmoid(lo), gl_ref)
    decay = jnp.exp(-jnp.exp(w))
    k_mod = k * (1.0 + (a - 1.0) * ka_ref[...])
    kk = k * kk_scale_ref[...]
    rk = r * k_mod * rk_ref[...]
    v_o[...] = v
    for q in range(RW_NQ):
        sl = slice(q * RW_QUAD, (q + 1) * RW_QUAD)
        kq = kk[:, sl]
        norm = jnp.maximum(jnp.sqrt(_seg_sum(kq * kq, ones2_ref[...])), 1e-12)
        kq = kq / norm
        put(kk_o, q, kq)
        put(b_o, q, kq * a[:, sl])
        vq = v[:, sl]
        bonus_o[:, sl] = _seg_sum(rk[:, sl], ones2_ref[...]) * vq
        put(w_o, q, decay[:, sl])
        put(r_o, q, r[:, sl])
        put(k_o, q, k_mod[:, sl])


def rwkv_pre(p_a, prev, boundary, T, W, *, tr=128):
    m = p_a.shape[0]
    tr = min(tr, m)
    assert (not boundary or T % tr == 0) and m % tr == 0
    padc = lambda t: jnp.pad(t, (0, RW_COLS_PAD - RW_COLS)).reshape(1, RW_COLS_PAD)
    vec = lambda t: t.reshape(1, D_RWKV)

    def lora_pad(w_up, row0):
        return jnp.zeros((LORA_COLS, D_RWKV), BF16).at[row0:row0 + w_up.shape[0]].set(w_up.astype(BF16))

    ones = _block_diag_ones(BF16)
    ones2 = jnp.concatenate([ones, ones], axis=0)
    row = pl.BlockSpec((tr, RW_COLS_PAD), lambda i: (i, 0))
    prev_spec = pl.BlockSpec((8, RW_COLS_PAD), lambda i: (i, 0)) if boundary else row
    const = lambda shape: pl.BlockSpec(shape, lambda i: (0,) * len(shape))
    out = pl.BlockSpec((tr, D_RWKV), lambda i: (i, 0))
    nlb = D_RWKV // LANES
    out_lb = pl.BlockSpec((nlb, tr, LANES), lambda i: (0, i, 0))
    return pl.pallas_call(
        functools.partial(_rwkv_pre_kernel, boundary=boundary),
        grid=(m // tr,),
        in_specs=[row, prev_spec, const((1, RW_COLS_PAD))] + [const((1, D_RWKV))] * 5
                 + [const((LORA_COLS, D_RWKV))] * 3 + [const((2 * RW_QUAD, RW_QUAD))],
        out_specs=[out_lb] * 5 + [out] * 3,
        out_shape=[jax.ShapeDtypeStruct((nlb, m, LANES), F32)] * 5 + [jax.ShapeDtypeStruct((m, D_RWKV), F32)] * 3,
        compiler_params=_cparams(("parallel",)),
        name="rwkv_pre",
    )(p_a, prev, padc(W['mu_shift']), vec(W['w0']), vec(W['a0']), vec(W['k_k']), vec(W['k_a']), vec(W['r_k']),
      lora_pad(W['w_lora_up'], 0), lora_pad(W['a_lora_up'], DECAY_LORA),
      lora_pad(W['g_lora_up'], DECAY_LORA + AAA_LORA), ones2)


SCAN_BATCH = 4


def _block_diag_ones(dtype):
    r2 = lax.broadcasted_iota(jnp.int32, (RW_QUAD, RW_QUAD), 0) // RW_HEAD
    c2 = lax.broadcasted_iota(jnp.int32, (RW_QUAD, RW_QUAD), 1) // RW_HEAD
    return (r2 == c2).astype(dtype)


def _seg_sum(x, ones2):
    hi = x.astype(BF16)
    lo = (x - hi.astype(F32)).astype(BF16)
    return jnp.dot(jnp.concatenate([hi, lo], axis=1), ones2, preferred_element_type=F32)


def _repeat_row(ref, lb, i, t):
    return ref[lb, i, pl.ds(t, RW_HEAD, stride=0), :]


def _seg_sum_xlane(x):
    first = lax.broadcasted_iota(jnp.int32, (1, LANES), 1) < RW_HEAD
    cols = []
    for c in range(x.shape[1] // LANES):
        xc = x[:, c * LANES:(c + 1) * LANES]
        sa = jnp.sum(jnp.where(first, xc, 0.0), axis=-1, keepdims=True)
        sb = jnp.sum(jnp.where(first, 0.0, xc), axis=-1, keepdims=True)
        cols.append(jnp.where(first, sa, sb))
    return jnp.concatenate(cols, axis=1)


V_HALF = 32


def _scan_kernel(w_ref, r_ref, k_ref, kk_ref, b_ref, v_ref, bonus_ref, g_ref, s0_ref,
                 tsel_ref, ones_ref, ones2_ref, lnw_ref, lnb_ref,
                 o_ref, sfin_ref, s_ref, y_ref, nat_ref, vt_ref, vtmp_ref, *, tc, nb):
    c = pl.program_id(1)
    n_half = (tc + V_HALF - 1) // V_HALF

    def head_block(i, h):
        q, h4 = divmod(h, 4)
        return (i, slice(q * RW_HEAD, (q + 1) * RW_HEAD), slice(h4 * RW_HEAD, (h4 + 1) * RW_HEAD))

    @pl.when(c == 0)
    def _():
        for i in range(nb):
            for h in range(RW_HEADS):
                s_ref[head_block(i, h)] = s0_ref[i, h]

    y_ref[...] = jnp.zeros(y_ref.shape, F32)
    lane_t = lax.broadcasted_iota(jnp.int32, (1, RW_QUAD), 1) % RW_HEAD

    for i in range(nb):
        v = v_ref[i]
        if tc < LANES:
            v = jnp.concatenate([v, jnp.zeros((LANES - tc, D_RWKV), F32)], axis=0)
        vt = v.T
        rows_i = slice(i * RW_ROWS, (i + 1) * RW_ROWS)
        for a in range(n_half):
            for q in range(RW_NQ):
                for h4 in range(4):
                    src = vt[(q * 4 + h4) * RW_HEAD:(q * 4 + h4 + 1) * RW_HEAD, a * V_HALF:(a + 1) * V_HALF]
                    vtmp_ref[q * RW_HEAD:(q + 1) * RW_HEAD, h4 * V_HALF:(h4 + 1) * V_HALF] = src
            tile = vtmp_ref[...]
            hi = tile.astype(BF16)
            vt_ref[a, rows_i, :LANES] = hi
            vt_ref[a, rows_i, LANES:] = (tile - hi.astype(F32)).astype(BF16)

    def stack(ref, i, t):
        rep = lambda lb: _repeat_row(ref, lb, i, t)
        return jnp.concatenate(
            [jnp.concatenate([rep(2 * q), rep(2 * q + 1)], axis=1) for q in range(RW_NQ)], axis=0)

    def body(t, carry, a):
        sel = jnp.where(tsel_ref[...] == t - a * V_HALF, 1.0, 0.0).astype(BF16)
        rows = lambda ref: jnp.concatenate([stack(ref, i, t) for i in range(nb)], axis=0)
        s = jnp.concatenate([s_ref[i] for i in range(nb)], axis=0)
        p = s * rows(kk_ref)
        xl = nb * RW_ROWS // 2
        s_kk = jnp.concatenate([_seg_sum_xlane(p[:xl]), _seg_sum(p[xl:], ones2_ref[...])], axis=0)
        v_col = jnp.dot(vt_ref[a], sel, preferred_element_type=F32)
        s = s * rows(w_ref) - s_kk * rows(b_ref) + v_col * rows(k_ref)
        for i in range(nb):
            s_ref[i] = s[i * RW_ROWS:(i + 1) * RW_ROWS]
        y_b = jnp.dot((s * rows(r_ref)).astype(BF16), ones_ref[...], preferred_element_type=F32)
        y_ref[...] = jnp.where(lane_t == t, y_b, y_ref[...])
        return carry

    for a in range(n_half):
        lax.fori_loop(a * V_HALF, min(tc, (a + 1) * V_HALF), functools.partial(body, a=a), 0, unroll=2)

    for i in range(nb):
        for q in range(RW_NQ):
            rq = slice(i * RW_ROWS + q * RW_HEAD, i * RW_ROWS + (q + 1) * RW_HEAD)
            yq = y_ref[rq, :]
            mu = jnp.mean(yq, axis=0, keepdims=True)
            d = yq - mu
            var = jnp.mean(d * d, axis=0, keepdims=True)
            y_ref[rq, :] = d * lax.rsqrt(var + GN_EPS)
        yt = y_ref[i * RW_ROWS:(i + 1) * RW_ROWS, :].T
        for q in range(RW_NQ):
            for h4 in range(4):
                ch = (q * 4 + h4) * RW_HEAD
                nat_ref[:, ch:ch + RW_HEAD] = yt[h4 * RW_HEAD:h4 * RW_HEAD + tc, q * RW_HEAD:(q + 1) * RW_HEAD]
        o = (nat_ref[...] * lnw_ref[...] + lnb_ref[...] + bonus_ref[i]) * g_ref[i]
        o_ref[i] = o.astype(o_ref.dtype)

    @pl.when(c == pl.num_programs(1) - 1)
    def _():
        for i in range(nb):
            for h in range(RW_HEADS):
                sfin_ref[i, h] = s_ref[head_block(i, h)]


def rwkv_scan(w, r, k, kk, b, v, bonus, g, s0, ln_w, ln_b):
    B, T, _ = bonus.shape
    tc = min(Y_TILE, T)
    nc = T // tc
    nb = SCAN_BATCH
    assert T % tc == 0 and B % nb == 0
    rr = lax.broadcasted_iota(jnp.int32, (RW_QUAD, RW_QUAD), 0)
    cc = lax.broadcasted_iota(jnp.int32, (RW_QUAD, RW_QUAD), 1)
    tsel = jnp.where((rr // V_HALF) % 4 == cc // RW_HEAD, rr % V_HALF, -1)
    ones = _block_diag_ones(BF16)
    ones2 = jnp.concatenate([ones, ones], axis=0)
    xspec = pl.BlockSpec((nb, tc, D_RWKV), lambda bi, ci: (bi, ci, 0))
    sspec = pl.BlockSpec((nb, RW_HEADS, RW_HEAD, RW_HEAD), lambda bi, ci: (bi, 0, 0, 0))
    const = lambda shape: pl.BlockSpec(shape, lambda bi, ci: (0,) * len(shape))
    o, s_fin = pl.pallas_call(
        functools.partial(_scan_kernel, tc=tc, nb=nb),
        grid=(B // nb, nc),
        in_specs=[pl.BlockSpec((D_RWKV // LANES, nb, tc, LANES), lambda bi, ci: (0, bi, ci, 0))] * 5
                 + [xspec] * 3 + [sspec, const((RW_QUAD, RW_QUAD)),
                                const((RW_QUAD, RW_QUAD)), const((2 * RW_QUAD, RW_QUAD)),
                                const((1, D_RWKV)), const((1, D_RWKV))],
        out_specs=[xspec, sspec],
        out_shape=[jax.ShapeDtypeStruct((B, T, D_RWKV), BF16),
                   jax.ShapeDtypeStruct((B, RW_HEADS, RW_HEAD, RW_HEAD), F32)],
        scratch_shapes=[pltpu.VMEM((nb, RW_ROWS, RW_QUAD), F32),
                        pltpu.VMEM((nb * RW_ROWS, RW_QUAD), F32), pltpu.VMEM((tc, D_RWKV), F32),
                        pltpu.VMEM(((tc + V_HALF - 1) // V_HALF, nb * RW_ROWS, RW_QUAD), BF16),
                        pltpu.VMEM((RW_ROWS, LANES), F32)],
        compiler_params=_cparams(("parallel", "arbitrary")),
        name="rwkv_scan",
    )(w, r, k, kk, b, v, bonus, g, s0, tsel, ones, ones2,
      ln_w.reshape(1, D_RWKV), ln_b.reshape(1, D_RWKV))
    return o, s_fin


ATT_SCALE = ATT_QK ** -0.5


def _lam_from(lamp_ref):
    lp = lamp_ref[...]
    return (jnp.exp(jnp.sum(lp[0:1] * lp[1:2])) - jnp.exp(jnp.sum(lp[2:3] * lp[3:4])) + LAM_INIT)


def _online_softmax_step(s, v_bf, m, l, a):
    m_new = jnp.maximum(m, jnp.max(s, axis=-1, keepdims=True))
    alpha = jnp.exp(m - m_new)
    p = jnp.exp(s - m_new)
    l = alpha * l + jnp.sum(p, axis=-1, keepdims=True)
    a = alpha * a + jnp.dot(p.astype(BF16), v_bf, preferred_element_type=F32)
    return m_new, l, a


def _subln(o, g):
    ms = jnp.mean(o * o, axis=-1, keepdims=True)
    return o * lax.rsqrt(ms + NORM_EPS) * g * (1.0 - LAM_INIT)


def _diff_prompt_kernel(slopes_ref, q_ref, k_ref, v_ref, lamp_ref, g_ref, o_ref, *, tq, tk):
    h = pl.program_id(1)
    qi = pl.program_id(2)
    slope = slopes_ref[h]
    lam = _lam_from(lamp_ref)
    q = q_ref[...]
    q1 = q[:, :ATT_QK].astype(BF16)
    q2 = q[:, ATT_QK:].astype(BF16)
    rel = (lax.broadcasted_iota(jnp.int32, (tq, tk), 0) - lax.broadcasted_iota(jnp.int32, (tq, tk), 1))
    base = -slope * rel.astype(F32)

    def kv_step(kb, carry, diagonal=False):
        m1, l1, a1, m2, l2, a2 = carry
        k0 = pl.multiple_of(kb * tk, tk)
        k = k_ref[pl.ds(k0, tk), :]
        v = v_ref[pl.ds(k0, tk), :].astype(BF16)
        bias = base - slope * (qi * tq - k0).astype(F32)
        if diagonal:
            bias = jnp.where(rel >= 0, bias, NEG_INF)
        s1 = lax.dot_general(q1, k[:, :ATT_QK].astype(BF16), DOT_NT, preferred_element_type=F32) * ATT_SCALE + bias
        s2 = lax.dot_general(q2, k[:, ATT_QK:].astype(BF16), DOT_NT, preferred_element_type=F32) * ATT_SCALE + bias
        m1, l1, a1 = _online_softmax_step(s1, v, m1, l1, a1)
        m2, l2, a2 = _online_softmax_step(s2, v, m2, l2, a2)
        return m1, l1, a1, m2, l2, a2

    col = lambda val: jnp.full((tq, 1), val, F32)
    acc0 = jnp.zeros((tq, ATT_V), F32)
    carry = lax.fori_loop(0, qi, kv_step, (col(NEG_INF), col(0.0), acc0, col(NEG_INF), col(0.0), acc0))
    m1, l1, a1, m2, l2, a2 = kv_step(qi, carry, diagonal=True)
    o = a1 / l1 - lam * (a2 / l2)
    o_ref[...] = _subln(o, g_ref[...]).astype(o_ref.dtype)


def diff_attn_prompt(q, k, v, B, T, slopes, lamp, g_subln, *, tq=512, tk=512):
    assert tq == tk and T % tq == 0
    nq = T // tq
    return pl.pallas_call(
        functools.partial(_diff_prompt_kernel, tq=tq, tk=tk),
        grid=(B, ATT_HEADS, nq),
        in_specs=[pl.BlockSpec(memory_space=pltpu.SMEM),
                  pl.BlockSpec((tq, ATT_V), lambda b, h, i: (b * nq + i, h)),
                  pl.BlockSpec((T, ATT_V), lambda b, h, i: (b, h)),
                  pl.BlockSpec((T, ATT_V), lambda b, h, i: (b, h)),
                  pl.BlockSpec((4, ATT_QK), lambda b, h, i: (0, 0)),
                  pl.BlockSpec((1, ATT_V), lambda b, h, i: (0, 0))],
        out_specs=pl.BlockSpec((tq, ATT_V), lambda b, h, i: (b * nq + i, h)),
        out_shape=jax.ShapeDtypeStruct((B * T, D_ATT), BF16),
        compiler_params=_cparams(("parallel", "parallel", "arbitrary")),
        name="diff_attn_prompt",
    )(slopes, q, k, v, lamp, g_subln.reshape(1, ATT_V))


PAGES_PER_STEP = 8
QROWS = 16


PAGE_ROWS = PAGE_SIZE * ATT_HEADS
DOT_TN = (((0,), (0,)), ((), ()))


def _diff_sample_kernel(pt_ref, slopes_ref, q_ref, kn_ref, vn_ref, lamp_ref, g_ref, *rest, ds, past):
    k_refs = rest[:PAGES_PER_STEP]
    v_refs = rest[PAGES_PER_STEP:2 * PAGES_PER_STEP]
    o_ref = rest[2 * PAGES_PER_STEP]
    qw_ref, bias_ref, m_ref, l_ref, a_ref = rest[2 * PAGES_PER_STEP + 1:]
    j = pl.program_id(1)
    cph = 2 * ds
    col = lax.broadcasted_iota(jnp.int32, (1, LANES), 1)
    col_head = col // cph
    col_q = col % ds
    slope_col = jnp.zeros((1, LANES), F32)
    for h in range(ATT_HEADS):
        slope_col = jnp.where(col_head == h, slopes_ref[h], slope_col)
    row_key = lax.broadcasted_iota(jnp.int32, (PAGE_ROWS, 1), 0) // ATT_HEADS
    row_head = lax.broadcasted_iota(jnp.int32, (PAGE_ROWS, 1), 0) % ATT_HEADS

    @pl.when(j == 0)
    def _():
        lane = lax.broadcasted_iota(jnp.int32, (ds, ATT_V), 1)
        rows = []
        for h in range(ATT_HEADS):
            qh = q_ref[0, :, h * ATT_V:(h + 1) * ATT_V]
            rows += [jnp.where(lane < ATT_QK, qh, 0.0), jnp.where(lane >= ATT_QK, qh, 0.0)]
        rows.append(jnp.zeros((LANES - ATT_HEADS * cph, ATT_V), F32))
        qw_ref[...] = jnp.concatenate(rows, axis=0).T.astype(BF16)
        bias_ref[...] = jnp.where(row_head == col_head, slope_col * row_key.astype(F32), NEG_INF)
        m_ref[...] = jnp.full(m_ref.shape, NEG_INF, F32)
        l_ref[...] = jnp.zeros(l_ref.shape, F32)
        a_ref[...] = jnp.zeros(a_ref.shape, F32)

    eye = (lax.broadcasted_iota(jnp.int32, (LANES, LANES), 0) == lax.broadcasted_iota(jnp.int32, (LANES, LANES), 1))
    to_col = lambda r: jnp.sum(jnp.where(eye, r, 0.0), axis=1, keepdims=True)

    def attend(k_list, v_list, key0, extra):
        s_pages = []
        for n, k_ref in enumerate(k_list):
            k2 = k_ref[0].reshape(PAGE_ROWS, ATT_V).astype(BF16)
            s = jnp.dot(k2, qw_ref[...], preferred_element_type=F32) * ATT_SCALE + bias_ref[...]
            s_pages.append(s + slope_col * (key0 + n * PAGE_SIZE - past - col_q).astype(F32))
        s = jnp.concatenate(s_pages, axis=0)
        if extra is not None:
            s = jnp.where(extra, s, NEG_INF)
        v2 = jnp.concatenate([v_ref[0].reshape(PAGE_ROWS, ATT_V).astype(BF16) for v_ref in v_list], axis=0)
        m_old = m_ref[...]
        m_new = jnp.maximum(m_old, jnp.max(s, axis=0, keepdims=True))
        alpha = jnp.exp(m_old - m_new)
        p = jnp.exp(s - m_new)
        l_ref[...] = alpha * l_ref[...] + jnp.sum(p, axis=0, keepdims=True)
        m_ref[...] = m_new
        pv = lax.dot_general(p.astype(BF16), v2, DOT_TN, preferred_element_type=F32)
        a_ref[...] = a_ref[...] * to_col(alpha) + pv

    attend(k_refs, v_refs, j * PAGES_PER_STEP * PAGE_SIZE, None)

    @pl.when(j == pl.num_programs(1) - 1)
    def _():
        attend([kn_ref], [vn_ref], past, (row_key < ds) & (row_key <= col_q))
        lam = _lam_from(lamp_ref)
        o_all = a_ref[...] / to_col(l_ref[...])
        for h in range(ATT_HEADS):
            o = o_all[h * cph:h * cph + ds] - lam * o_all[h * cph + ds:(h + 1) * cph]
            o_ref[0, :, h * ATT_V:(h + 1) * ATT_V] = _subln(o, g_ref[...]).astype(o_ref.dtype)


def diff_attn_sample(q, k_new, v_new, cache_k, cache_v, page_table, slopes, lamp, g_subln):
    B, ds, _ = q.shape
    n_pages = page_table.shape[1]
    past = n_pages * PAGE_SIZE
    assert n_pages % PAGES_PER_STEP == 0 and ATT_HEADS * 2 * ds <= LANES
    pad = lambda t: jnp.pad(t, ((0, 0), (0, PAGE_SIZE - ds), (0, 0), (0, 0)))
    page_blk = (1, PAGE_SIZE, ATT_HEADS, ATT_V)

    def page_spec(i):
        return pl.BlockSpec(page_blk, lambda b, j, pt: (pt[b, j * PAGES_PER_STEP + i], 0, 0, 0))

    fixed = lambda shape: pl.BlockSpec(shape, lambda b, j, pt: (0,) * len(shape))
    grid_spec = pltpu.PrefetchScalarGridSpec(
        num_scalar_prefetch=1,
        grid=(B, n_pages // PAGES_PER_STEP),
        in_specs=[pl.BlockSpec(memory_space=pltpu.SMEM),
                  pl.BlockSpec((1, ds, D_ATT), lambda b, j, pt: (b, 0, 0)),
                  pl.BlockSpec(page_blk, lambda b, j, pt: (b, 0, 0, 0)),
                  pl.BlockSpec(page_blk, lambda b, j, pt: (b, 0, 0, 0)),
                  fixed((4, ATT_QK)), fixed((1, ATT_V))]
                 + [page_spec(i) for i in range(PAGES_PER_STEP)] * 2,
        out_specs=pl.BlockSpec((1, ds, D_ATT), lambda b, j, pt: (b, 0, 0)),
        scratch_shapes=[pltpu.VMEM((ATT_V, LANES), BF16),
                        pltpu.VMEM((PAGE_ROWS, LANES), F32),
                        pltpu.VMEM((1, LANES), F32),
                        pltpu.VMEM((1, LANES), F32),
                        pltpu.VMEM((LANES, ATT_V), F32)])
    return pl.pallas_call(
        functools.partial(_diff_sample_kernel, ds=ds, past=past),
        grid_spec=grid_spec,
        out_shape=jax.ShapeDtypeStruct((B, ds, D_ATT), BF16),
        compiler_params=_cparams(("parallel", "arbitrary")),
        name="diff_attn_sample",
    )(page_table, slopes, q, pad(k_new), pad(v_new), lamp, g_subln.reshape(1, ATT_V),
      *([cache_k] * PAGES_PER_STEP), *([cache_v] * PAGES_PER_STEP))


def _mem_attn_kernel(q_ref, k_ref, v_ref, o_ref, *, paged_heads, rows):
    for h in range(MEM_HEADS):
        sl = slice(h * MEM_HEAD, (h + 1) * MEM_HEAD)
        q = q_ref[:, sl] if q_ref.ndim == 2 else q_ref[0, :, sl]
        if q.shape[0] < rows:
            q = jnp.concatenate([q, jnp.zeros((rows - q.shape[0], MEM_HEAD), F32)], axis=0)
        k = k_ref[0, :, h, :] if paged_heads else k_ref[:, sl]
        v = v_ref[0, :, h, :] if paged_heads else v_ref[:, sl]
        s = lax.dot_general(q.astype(BF16), k.astype(BF16), DOT_NT, preferred_element_type=F32) * (MEM_HEAD ** -0.5)
        p = jnp.exp(s - jnp.max(s, axis=-1, keepdims=True))
        o = jnp.dot(p.astype(BF16), v.astype(BF16), preferred_element_type=F32) / jnp.sum(p, axis=-1, keepdims=True)
        if q_ref.ndim == 2:
            o_ref[:, sl] = o.astype(o_ref.dtype)
        else:
            o_ref[0, :, sl] = o[:q_ref.shape[1]].astype(o_ref.dtype)


def mem_attn_prompt(p_c, kv, B, T, MT, *, tq=512):
    tq = min(tq, T)
    nq = T // tq
    return pl.pallas_call(
        functools.partial(_mem_attn_kernel, paged_heads=False, rows=tq),
        grid=(B, nq),
        in_specs=[pl.BlockSpec((tq, D_MEM), lambda b, i: (b * nq + i, 0)),
                  pl.BlockSpec((MT, D_MEM), lambda b, i: (b, 0)),
                  pl.BlockSpec((MT, D_MEM), lambda b, i: (b, 1))],
        out_specs=pl.BlockSpec((tq, D_MEM), lambda b, i: (b * nq + i, 0)),
        out_shape=jax.ShapeDtypeStruct((B * T, D_MEM), BF16),
        compiler_params=_cparams(("parallel", "parallel")),
        name="mem_attn_prompt",
    )(p_c, kv, kv)


def mem_attn_sample(qm, mem_k, mem_v):
    B, ds, _ = qm.shape
    MT = mem_k.shape[1]
    kv_spec = pl.BlockSpec((1, MT, MEM_HEADS, MEM_HEAD), lambda b: (b, 0, 0, 0))
    return pl.pallas_call(
        functools.partial(_mem_attn_kernel, paged_heads=True, rows=QROWS),
        grid=(B,),
        in_specs=[pl.BlockSpec((1, ds, D_MEM), lambda b: (b, 0, 0)), kv_spec, kv_spec],
        out_specs=pl.BlockSpec((1, ds, D_MEM), lambda b: (b, 0, 0)),
        out_shape=jax.ShapeDtypeStruct((B, ds, D_MEM), BF16),
        compiler_params=_cparams(("parallel",)),
        name="mem_attn_sample",
    )(qm, mem_k, mem_v)


PRE_ROWS = 128


def _rwkv7_branch(p_a, B, T, shift_prev, s0, W):
    M = B * T
    sp = jnp.pad(shift_prev, ((0, 0), (0, RW_COLS_PAD - RW_COLS)))
    if T % PRE_ROWS == 0:
        first = jnp.arange(M // PRE_ROWS) * PRE_ROWS
        rows = jnp.where((first % T == 0)[:, None], sp[first // T], p_a[jnp.maximum(first - 1, 0)])
        prev = jnp.zeros((M // PRE_ROWS, 8, RW_COLS_PAD), F32).at[:, 0].set(rows).reshape(-1, RW_COLS_PAD)
        boundary = True
    else:
        p3 = p_a.reshape(B, T, RW_COLS_PAD)
        prev = jnp.concatenate([sp[:, None], p3[:, :-1]], axis=1).reshape(M, RW_COLS_PAD)
        boundary = False
    outs = rwkv_pre(p_a, prev, boundary, T, W, tr=PRE_ROWS)
    per_step = [t.reshape(D_RWKV // LANES, B, T, LANES) for t in outs[:5]]
    v, bonus, g = (t.reshape(B, T, D_RWKV) for t in outs[5:])
    o, s_fin = rwkv_scan(*per_step, v, bonus, g, s0, W['ln_x_w'], W['ln_x_b'])
    shift_last = p_a.reshape(B, T, RW_COLS_PAD)[:, -1, :RW_COLS]
    return o.reshape(M, D_RWKV), s_fin, shift_last


def _layer(x, shift_prev, s0, attend, mem_attend, W):
    B, T, _ = x.shape
    M = B * T
    x2 = x.reshape(M, D_MODEL)
    h = rmsnorm_bf16(x2, W['g_pre_mix'])
    wt = W['w_in_t']
    p_a = matmul_nt(h, wt, n_cols=RW_COLS_PAD)
    q = matmul_nt(h, wt, row0=RW_COLS, n_cols=D_ATT)
    k = matmul_nt(h, wt, row0=RW_COLS + D_ATT, n_cols=D_ATT)
    v = matmul_nt(h, wt, row0=RW_COLS + 2 * D_ATT, n_cols=D_ATT)
    p_c = matmul_nt(h, wt, row0=RW_COLS + 3 * D_ATT, n_cols=D_MEM + 3 * D_MODEL)

    o_rw, s_fin, shift_last = _rwkv7_branch(p_a, B, T, shift_prev, s0, W)
    o_att = attend(q, k, v)
    o_mem = mem_attend(p_c)

    m = gated_merge(o_rw, o_att, o_mem,
                    W['w_br_rwkv'], W['w_br_attn'], W['w_br_mem'], p_c, D_MEM)
    k = k.reshape(B, T, ATT_HEADS, 2 * ATT_QK)
    v = v.reshape(B, T, ATT_HEADS, ATT_V)
    z = matmul(m, W['w_out'])
    x1, h2 = norm_residual(z, W['g_post_mix'], x2, W['g_pre_mlp'])
    u = matmul(h2, W['w_up'], out_dtype=BF16, relu2=True)
    z2 = matmul(u, W['w_down'], tm=2048, tk=2048)
    y = norm_residual(z2, W['g_post_mlp'], x1)
    return y.reshape(B, T, D_MODEL), k, v, s_fin, shift_last


def kernel(x_prompt, x_sample, cache_k, cache_v, page_table, state_wkv, state_shift, cache_mem_k, cache_mem_v, mem_prompt, g_pre_mix, w_in, mu_shift, w0, w_lora_up, a0, a_lora_up, g_lora_up, k_k, k_a, r_k, ln_x_w, ln_x_b, lambda_q1, lambda_k1, lambda_q2, lambda_k2, g_subln, g_mem, w_mem_kv, w_br_rwkv, w_br_attn, w_br_mem, w_out, g_post_mix, g_pre_mlp, w_up, w_down, g_post_mlp):
    W = dict(g_pre_mix=g_pre_mix, w_in_t=w_in.T,
             mu_shift=mu_shift, w0=w0, w_lora_up=w_lora_up,
             a0=a0, a_lora_up=a_lora_up, g_lora_up=g_lora_up, k_k=k_k, k_a=k_a, r_k=r_k,
             ln_x_w=ln_x_w, ln_x_b=ln_x_b, lambda_q1=lambda_q1, lambda_k1=lambda_k1,
             lambda_q2=lambda_q2, lambda_k2=lambda_k2, g_subln=g_subln, w_br_rwkv=w_br_rwkv.astype(BF16),
             w_br_attn=w_br_attn.astype(BF16), w_br_mem=w_br_mem.astype(BF16), w_out=w_out,
             g_post_mix=g_post_mix,
             g_pre_mlp=g_pre_mlp, w_up=w_up, w_down=w_down, g_post_mlp=g_post_mlp)
    slopes = jnp.asarray([2.0 ** (-8.0 * (h + 1) / ATT_HEADS) for h in range(ATT_HEADS)], dtype=F32)
    lamp = jnp.stack([lambda_q1, lambda_k1, lambda_q2, lambda_k2])
    B, T, _ = x_prompt.shape
    BS, DS, _ = x_sample.shape
    MT = mem_prompt.shape[1]

    hm = rmsnorm_bf16(mem_prompt.reshape(B * MT, D_MODEL), g_mem)
    kv = matmul(hm, w_mem_kv)
    mem_k_prompt = kv[:, :D_MEM].reshape(B, MT, MEM_HEADS, MEM_HEAD)
    mem_v_prompt = kv[:, D_MEM:].reshape(B, MT, MEM_HEADS, MEM_HEAD)

    y_prompt, k_prompt, v_prompt, wkv_prompt, shift_prompt = _layer(
        x_prompt, jnp.zeros((B, RW_COLS), F32), jnp.zeros((B, RW_HEADS, RW_HEAD, RW_HEAD), F32),
        lambda q, k, v: diff_attn_prompt(q, k, v, B, T, slopes, lamp, g_subln),
        lambda p_c: mem_attn_prompt(p_c, kv, B, T, MT), W)

    def attend_sample(q, k, v):
        page = lambda t: t.reshape(BS, DS, ATT_HEADS, ATT_V)
        o = diff_attn_sample(q.reshape(BS, DS, D_ATT), page(k), page(v), cache_k, cache_v, page_table,
                             slopes, lamp, g_subln)
        return o.reshape(BS * DS, D_ATT)

    def mem_attend_sample(p_c):
        qm = p_c[:, :D_MEM].reshape(BS, DS, D_MEM)
        return mem_attn_sample(qm, cache_mem_k, cache_mem_v).reshape(BS * DS, D_MEM)

    y_sample, k_sample, v_sample, wkv_sample, shift_sample = _layer(
        x_sample, state_shift, state_wkv, attend_sample, mem_attend_sample, W)
    return (y_prompt, y_sample, k_prompt, v_prompt, k_sample, v_sample,
            wkv_prompt, wkv_sample, shift_prompt, shift_sample, mem_k_prompt, mem_v_prompt)
```

```python
import functools
import math

import jax
import jax.numpy as jnp
from jax import lax
from jax.experimental import pallas as pl
from jax.experimental.pallas import tpu as pltpu

F32 = jnp.float32
BF16 = jnp.bfloat16

D_MODEL = 4096
PAGE_SIZE = 128
D_RWKV = 2048
RW_HEAD = 64
RW_HEADS = 32
DECAY_LORA = 96
AAA_LORA = 96
GATE_LORA = 256
RW_COLS = 3 * D_RWKV + DECAY_LORA + AAA_LORA + GATE_LORA
RW_COLS_PAD = 6656
GN_EPS = 64e-5
D_ATT = 2048
ATT_HEADS = 8
ATT_QK = 128
ATT_V = 256
MEM_HEADS = 4
MEM_HEAD = 128
D_MEM = 512
NORM_EPS = 1e-6
NEG_INF = -1e30
LAM_INIT = 0.8 - 0.6 * math.exp(-0.3 * 0)

VMEM_LIMIT = 56 * 1024 * 1024
LANES = 128


def _cparams(sem):
    return pltpu.CompilerParams(dimension_semantics=sem, vmem_limit_bytes=VMEM_LIMIT)


def _rmsnorm_kernel(x_ref, g_ref, o_ref):
    x = x_ref[...]
    ms = jnp.mean(x * x, axis=-1, keepdims=True)
    o_ref[...] = (x * lax.rsqrt(ms + NORM_EPS) * g_ref[...]).astype(o_ref.dtype)


def rmsnorm_bf16(x, g, tr=256):
    m, d = x.shape
    tr = min(tr, m)
    return pl.pallas_call(
        _rmsnorm_kernel,
        grid=(m // tr,),
        in_specs=[pl.BlockSpec((tr, d), lambda i: (i, 0)),
                  pl.BlockSpec((1, d), lambda i: (0, 0))],
        out_specs=pl.BlockSpec((tr, d), lambda i: (i, 0)),
        out_shape=jax.ShapeDtypeStruct((m, d), BF16),
        compiler_params=_cparams(("parallel",)),
        name="rmsnorm_bf16",
    )(x, g.reshape(1, d))


DOT_NT = (((1,), (1,)), ((), ()))


def _mm_nt_kernel(a_ref, bt_ref, o_ref):
    o_ref[...] = lax.dot_general(a_ref[...], bt_ref[...].astype(BF16), DOT_NT,
                                 preferred_element_type=F32).astype(o_ref.dtype)


def matmul_nt(a, bt, *, row0=0, n_cols, tm=2048, tn=512):
    m, kd = a.shape
    tm = min(tm, m)
    assert m % tm == 0 and n_cols % tn == 0 and row0 % 8 == 0 and bt.shape[1] == kd
    return pl.pallas_call(
        _mm_nt_kernel,
        grid=(m // tm, n_cols // tn),
        in_specs=[pl.BlockSpec((tm, kd), lambda i, j: (i, 0), pipeline_mode=pl.Buffered(1)),
                  pl.BlockSpec((pl.Element(tn), pl.Element(kd)),
                               lambda i, j: ((row0 // 8 + j * (tn // 8)) * 8, 0))],
        out_specs=pl.BlockSpec((tm, tn), lambda i, j: (i, j)),
        out_shape=jax.ShapeDtypeStruct((m, n_cols), F32),
        compiler_params=_cparams(("parallel", "parallel")),
        name="matmul_nt",
    )(a, bt)


def _mm_nt_multi_kernel(a_ref, bt_ref, *o_refs, starts):
    j = pl.program_id(1)
    r = lax.dot_general(a_ref[...], bt_ref[...].astype(BF16), DOT_NT, preferred_element_type=F32)
    for s, o_ref in enumerate(o_refs):
        @pl.when((j >= starts[s]) & (j < starts[s + 1]))
        def _():
            o_ref[...] = r


MULTI_TM = 1024


def matmul_nt_multi(a, bt, segments, *, tm=MULTI_TM, tn=512):
    m, kd = a.shape
    tm = min(tm, m)
    assert m % tm == 0 and all(r0 % 8 == 0 and n % tn == 0 for r0, n in segments) and bt.shape[1] == kd
    starts = [0]
    for _, n in segments:
        starts.append(starts[-1] + n // tn)

    def bt_row8(j):
        r8 = j * (tn // 8)
        for s, (r0, _) in enumerate(segments):
            prev_end = 0 if s == 0 else segments[s - 1][0] + segments[s - 1][1]
            r8 = r8 + jnp.where(j >= starts[s], (r0 - prev_end) // 8, 0)
        return r8

    def out_spec(s):
        return pl.BlockSpec((tm, tn), lambda i, j: (i, jnp.clip(j - starts[s], 0, starts[s + 1] - starts[s] - 1)))

    return pl.pallas_call(
        functools.partial(_mm_nt_multi_kernel, starts=tuple(starts)),
        grid=(m // tm, starts[-1]),
        in_specs=[pl.BlockSpec((tm, kd), lambda i, j: (i, 0), pipeline_mode=pl.Buffered(1)),
                  pl.BlockSpec((pl.Element(tn), pl.Element(kd)), lambda i, j: (bt_row8(j) * 8, 0))],
        out_specs=[out_spec(s) for s in range(len(segments))],
        out_shape=[jax.ShapeDtypeStruct((m, n), F32) for _, n in segments],
        compiler_params=_cparams(("parallel", "arbitrary")),
        name="matmul_nt_multi",
    )(a, bt)


def _mm_kernel(a_ref, b_ref, o_ref, *scratch, relu2, nk):
    def finish(r):
        if relu2:
            r = jnp.square(jnp.maximum(r, 0.0))
        o_ref[...] = r.astype(o_ref.dtype)

    part = jnp.dot(a_ref[...], b_ref[...].astype(BF16), preferred_element_type=F32)
    if nk == 1:
        finish(part)
        return
    acc_ref, = scratch
    k = pl.program_id(2)

    @pl.when(k == 0)
    def _():
        acc_ref[...] = part

    @pl.when(k > 0)
    def _():
        acc_ref[...] += part

    @pl.when(k == nk - 1)
    def _():
        finish(acc_ref[...])


def matmul(a, b, *, tm=2048, tn=512, tk=4096, out_dtype=F32, relu2=False):
    m, kd = a.shape
    n = b.shape[1]
    tm = min(tm, m)
    tk = min(tk, kd)
    assert m % tm == 0 and n % tn == 0 and kd % tk == 0
    nk = kd // tk
    a_mode = dict(pipeline_mode=pl.Buffered(1)) if nk == 1 else {}
    return pl.pallas_call(
        functools.partial(_mm_kernel, relu2=relu2, nk=nk),
        grid=(m // tm, n // tn, nk),
        in_specs=[pl.BlockSpec((tm, tk), lambda i, j, k: (i, k), **a_mode),
                  pl.BlockSpec((tk, tn), lambda i, j, k: (k, j))],
        out_specs=pl.BlockSpec((tm, tn), lambda i, j, k: (i, j)),
        out_shape=jax.ShapeDtypeStruct((m, n), out_dtype),
        scratch_shapes=[pltpu.VMEM((tm, tn), F32)] if nk > 1 else [],
        compiler_params=_cparams(("parallel", "parallel", "arbitrary")),
        name="matmul",
    )(a, b)


def _norm_res_kernel(z_ref, g_ref, x_ref, *rest, with_next):
    z = z_ref[...]
    ms = jnp.mean(z * z, axis=-1, keepdims=True)
    y = x_ref[...] + z * lax.rsqrt(ms + NORM_EPS) * g_ref[...]
    if with_next:
        g2_ref, o_ref, h_ref = rest
        ms2 = jnp.mean(y * y, axis=-1, keepdims=True)
        h_ref[...] = (y * lax.rsqrt(ms2 + NORM_EPS) * g2_ref[...]).astype(h_ref.dtype)
    else:
        o_ref, = rest
    o_ref[...] = y


def norm_residual(z, g, x, g_next=None, tr=256):
    m, d = z.shape
    tr = min(tr, m)
    row = pl.BlockSpec((tr, d), lambda i: (i, 0))
    vec = pl.BlockSpec((1, d), lambda i: (0, 0))
    with_next = g_next is not None
    args = (z, g.reshape(1, d), x) + ((g_next.reshape(1, d),) if with_next else ())
    return pl.pallas_call(
        functools.partial(_norm_res_kernel, with_next=with_next),
        grid=(m // tr,),
        in_specs=[row, vec, row] + ([vec] if with_next else []),
        out_specs=[row, row] if with_next else row,
        out_shape=([jax.ShapeDtypeStruct((m, d), F32), jax.ShapeDtypeStruct((m, d), BF16)] if with_next
                   else jax.ShapeDtypeStruct((m, d), F32)),
        compiler_params=_cparams(("parallel",)),
        name="norm_residual",
    )(*args)


def _merge_kernel(orw_ref, oatt_ref, omem_ref, wr_ref, wa_ref, wm_ref, grw_ref, gatt_ref, gmem_ref, o_ref):
    def branch(o_ref_, w_ref_, g_ref_):
        y = jnp.dot(o_ref_[...], w_ref_[...], preferred_element_type=F32)
        return jax.nn.sigmoid(g_ref_[...]) * y

    m = branch(orw_ref, wr_ref, grw_ref) + branch(oatt_ref, wa_ref, gatt_ref) + branch(omem_ref, wm_ref, gmem_ref)
    o_ref[...] = m.astype(o_ref.dtype)


def gated_merge(o_rw, o_att, o_mem, w_r, w_a, w_m, p_b, gate_col0, *, tm=1024, tn=512):
    m = o_rw.shape[0]
    tm = min(tm, m)
    gb = gate_col0 // tn
    nb = D_MODEL // tn
    assert gate_col0 % tn == 0

    def gate_spec(b):
        return pl.BlockSpec((tm, tn), lambda i, j: (i, gb + b * nb + j))

    return pl.pallas_call(
        _merge_kernel,
        grid=(m // tm, D_MODEL // tn),
        in_specs=[pl.BlockSpec((tm, D_RWKV), lambda i, j: (i, 0)),
                  pl.BlockSpec((tm, D_ATT), lambda i, j: (i, 0)),
                  pl.BlockSpec((tm, D_MEM), lambda i, j: (i, 0)),
                  pl.BlockSpec((D_RWKV, tn), lambda i, j: (0, j)),
                  pl.BlockSpec((D_ATT, tn), lambda i, j: (0, j)),
                  pl.BlockSpec((D_MEM, tn), lambda i, j: (0, j)),
                  gate_spec(0), gate_spec(1), gate_spec(2)],
        out_specs=pl.BlockSpec((tm, tn), lambda i, j: (i, j)),
        out_shape=jax.ShapeDtypeStruct((m, D_MODEL), BF16),
        compiler_params=_cparams(("parallel", "parallel")),
        name="gated_merge",
    )(o_rw, o_att, o_mem, w_r, w_a, w_m, p_b, p_b, p_b)


RW_QUAD = 4 * RW_HEAD
RW_NQ = D_RWKV // RW_QUAD
RW_ROWS = RW_NQ * RW_HEAD
Y_TILE = 64


LORA_COLS = RW_COLS_PAD - 3 * D_RWKV


def _rwkv_pre_kernel(p_ref, prev_ref, mu_ref, w0_ref, a0_ref, kk_scale_ref, ka_ref, rk_ref,
                     wl_ref, al_ref, gl_ref, ones2_ref,
                     w_o, r_o, k_o, kk_o, b_o, v_o, bonus_o, g_o, *, boundary):
    p = p_ref[...]
    rows = p.shape[0]
    if boundary:
        row_id = lax.broadcasted_iota(jnp.int32, (rows, 1), 0)
        prev = jnp.where(row_id == 0, prev_ref[0:1, :], pltpu.roll(p, 1, 0))
    else:
        prev = prev_ref[...]
    xs = p + (prev - p) * mu_ref[...]
    r = xs[:, 0:D_RWKV]
    k = xs[:, D_RWKV:2 * D_RWKV]
    v = xs[:, 2 * D_RWKV:3 * D_RWKV]
    lo = xs[:, 3 * D_RWKV:]
    mm = lambda x, w_ref_: jnp.dot(x.astype(BF16), w_ref_[...], preferred_element_type=F32)
    w = -jax.nn.softplus(-(w0_ref[...] + mm(jnp.tanh(lo), wl_ref))) - 0.5
    a = jax.nn.sigmoid(a0_ref[...] + mm(lo, al_ref))
    def put(o_ref, q, x):
        o_ref[2 * q] = x[:, :LANES]
        o_ref[2 * q + 1] = x[:, LANES:]

    g_o[...] = mm(jax.nn.sigmoid(lo), gl_ref)
    decay = jnp.exp(-jnp.exp(w))
    k_mod = k * (1.0 + (a - 1.0) * ka_ref[...])
    kk = k * kk_scale_ref[...]
    rk = r * k_mod * rk_ref[...]
    v_o[...] = v
    for q in range(RW_NQ):
        sl = slice(q * RW_QUAD, (q + 1) * RW_QUAD)
        kq = kk[:, sl]
        norm = jnp.maximum(jnp.sqrt(_seg_sum(kq * kq, ones2_ref[...])), 1e-12)
        kq = kq / norm
        put(kk_o, q, kq)
        put(b_o, q, kq * a[:, sl])
        vq = v[:, sl]
        bonus_o[:, sl] = _seg_sum(rk[:, sl], ones2_ref[...]) * vq
        put(w_o, q, decay[:, sl])
        put(r_o, q, r[:, sl])
        put(k_o, q, k_mod[:, sl])


def rwkv_pre(p_a, prev, boundary, T, W, *, tr=128):
    m = p_a.shape[0]
    tr = min(tr, m)
    assert (not boundary or T % tr == 0) and m % tr == 0
    padc = lambda t: jnp.pad(t, (0, RW_COLS_PAD - RW_COLS)).reshape(1, RW_COLS_PAD)
    vec = lambda t: t.reshape(1, D_RWKV)

    def lora_pad(w_up, row0):
        return jnp.zeros((LORA_COLS, D_RWKV), BF16).at[row0:row0 + w_up.shape[0]].set(w_up.astype(BF16))

    ones = _block_diag_ones(BF16)
    ones2 = jnp.concatenate([ones, ones], axis=0)
    row = pl.BlockSpec((tr, RW_COLS_PAD), lambda i: (i, 0))
    prev_spec = pl.BlockSpec((8, RW_COLS_PAD), lambda i: (i, 0)) if boundary else row
    const = lambda shape: pl.BlockSpec(shape, lambda i: (0,) * len(shape))
    out = pl.BlockSpec((tr, D_RWKV), lambda i: (i, 0))
    nlb = D_RWKV // LANES
    out_lb = pl.BlockSpec((nlb, tr, LANES), lambda i: (0, i, 0))
    return pl.pallas_call(
        functools.partial(_rwkv_pre_kernel, boundary=boundary),
        grid=(m // tr,),
        in_specs=[row, prev_spec, const((1, RW_COLS_PAD))] + [const((1, D_RWKV))] * 5
                 + [const((LORA_COLS, D_RWKV))] * 3 + [const((2 * RW_QUAD, RW_QUAD))],
        out_specs=[out_lb] * 5 + [out] * 3,
        out_shape=[jax.ShapeDtypeStruct((nlb, m, LANES), F32)] * 5 + [jax.ShapeDtypeStruct((m, D_RWKV), F32)] * 3,
        compiler_params=_cparams(("parallel",)),
        name="rwkv_pre",
    )(p_a, prev, padc(W['mu_shift']), vec(W['w0']), vec(W['a0']), vec(W['k_k']), vec(W['k_a']), vec(W['r_k']),
      lora_pad(W['w_lora_up'], 0), lora_pad(W['a_lora_up'], DECAY_LORA),
      lora_pad(W['g_lora_up'], DECAY_LORA + AAA_LORA), ones2)


SCAN_BATCH = 4


def _block_diag_ones(dtype):
    r2 = lax.broadcasted_iota(jnp.int32, (RW_QUAD, RW_QUAD), 0) // RW_HEAD
    c2 = lax.broadcasted_iota(jnp.int32, (RW_QUAD, RW_QUAD), 1) // RW_HEAD
    return (r2 == c2).astype(dtype)


def _seg_sum(x, ones2):
    hi = x.astype(BF16)
    lo = (x - hi.astype(F32)).astype(BF16)
    return jnp.dot(jnp.concatenate([hi, lo], axis=1), ones2, preferred_element_type=F32)


def _repeat_row(ref, lb, i, t):
    return ref[lb, i, pl.ds(t, RW_HEAD, stride=0), :]


def _seg_sum_xlane(x):
    first = lax.broadcasted_iota(jnp.int32, (1, LANES), 1) < RW_HEAD
    cols = []
    for c in range(x.shape[1] // LANES):
        xc = x[:, c * LANES:(c + 1) * LANES]
        sa = jnp.sum(jnp.where(first, xc, 0.0), axis=-1, keepdims=True)
        sb = jnp.sum(jnp.where(first, 0.0, xc), axis=-1, keepdims=True)
        cols.append(jnp.where(first, sa, sb))
    return jnp.concatenate(cols, axis=1)


V_HALF = 32


def _scan_kernel(w_ref, r_ref, k_ref, kk_ref, b_ref, v_ref, bonus_ref, g_ref, s0_ref,
                 tsel_ref, ones_ref, ones2_ref, lnw_ref, lnb_ref,
                 o_ref, sfin_ref, s_ref, y_ref, nat_ref, vt_ref, vtmp_ref, *, tc, nb):
    c = pl.program_id(1)
    n_half = (tc + V_HALF - 1) // V_HALF

    def head_block(i, h):
        q, h4 = divmod(h, 4)
        return (i, slice(q * RW_HEAD, (q + 1) * RW_HEAD), slice(h4 * RW_HEAD, (h4 + 1) * RW_HEAD))

    @pl.when(c == 0)
    def _():
        for i in range(nb):
            for h in range(RW_HEADS):
                s_ref[head_block(i, h)] = s0_ref[i, h]

    y_ref[...] = jnp.zeros(y_ref.shape, F32)
    lane_t = lax.broadcasted_iota(jnp.int32, (1, RW_QUAD), 1) % RW_HEAD

    for i in range(nb):
        v = v_ref[i]
        if tc < LANES:
            v = jnp.concatenate([v, jnp.zeros((LANES - tc, D_RWKV), F32)], axis=0)
        vt = v.T
        rows_i = slice(i * RW_ROWS, (i + 1) * RW_ROWS)
        for a in range(n_half):
            for q in range(RW_NQ):
                for h4 in range(4):
                    src = vt[(q * 4 + h4) * RW_HEAD:(q * 4 + h4 + 1) * RW_HEAD, a * V_HALF:(a + 1) * V_HALF]
                    vtmp_ref[q * RW_HEAD:(q + 1) * RW_HEAD, h4 * V_HALF:(h4 + 1) * V_HALF] = src
            tile = vtmp_ref[...]
            hi = tile.astype(BF16)
            vt_ref[a, rows_i, :LANES] = hi
            vt_ref[a, rows_i, LANES:] = (tile - hi.astype(F32)).astype(BF16)

    def stack(ref, i, t):
        rep = lambda lb: _repeat_row(ref, lb, i, t)
        return jnp.concatenate(
            [jnp.concatenate([rep(2 * q), rep(2 * q + 1)], axis=1) for q in range(RW_NQ)], axis=0)

    def body(t, carry, a):
        sel = jnp.where(tsel_ref[...] == t - a * V_HALF, 1.0, 0.0).astype(BF16)
        rows = lambda ref: jnp.concatenate([stack(ref, i, t) for i in range(nb)], axis=0)
        s = jnp.concatenate([s_ref[i] for i in range(nb)], axis=0)
        p = s * rows(kk_ref)
        xl = nb * RW_ROWS // 2
        s_kk = jnp.concatenate([_seg_sum_xlane(p[:xl]), _seg_sum(p[xl:], ones2_ref[...])], axis=0)
        v_col = jnp.dot(vt_ref[a], sel, preferred_element_type=F32)
        s = s * rows(w_ref) - s_kk * rows(b_ref) + v_col * rows(k_ref)
        for i in range(nb):
            s_ref[i] = s[i * RW_ROWS:(i + 1) * RW_ROWS]
        y_b = jnp.dot((s * rows(r_ref)).astype(BF16), ones_ref[...], preferred_element_type=F32)
        y_ref[...] = jnp.where(lane_t == t, y_b, y_ref[...])
        return carry

    for a in range(n_half):
        lax.fori_loop(a * V_HALF, min(tc, (a + 1) * V_HALF), functools.partial(body, a=a), 0, unroll=2)

    for i in range(nb):
        for q in range(RW_NQ):
            rq = slice(i * RW_ROWS + q * RW_HEAD, i * RW_ROWS + (q + 1) * RW_HEAD)
            yq = y_ref[rq, :]
            mu = jnp.mean(yq, axis=0, keepdims=True)
            d = yq - mu
            var = jnp.mean(d * d, axis=0, keepdims=True)
            y_ref[rq, :] = d * lax.rsqrt(var + GN_EPS)
        yt = y_ref[i * RW_ROWS:(i + 1) * RW_ROWS, :].T
        for q in range(RW_NQ):
            for h4 in range(4):
                ch = (q * 4 + h4) * RW_HEAD
                nat_ref[:, ch:ch + RW_HEAD] = yt[h4 * RW_HEAD:h4 * RW_HEAD + tc, q * RW_HEAD:(q + 1) * RW_HEAD]
        o = (nat_ref[...] * lnw_ref[...] + lnb_ref[...] + bonus_ref[i]) * g_ref[i]
        o_ref[i] = o.astype(o_ref.dtype)

    @pl.when(c == pl.num_programs(1) - 1)
    def _():
        for i in range(nb):
            for h in range(RW_HEADS):
                sfin_ref[i, h] = s_ref[head_block(i, h)]


def rwkv_scan(w, r, k, kk, b, v, bonus, g, s0, ln_w, ln_b):
    B, T, _ = bonus.shape
    tc = min(Y_TILE, T)
    nc = T // tc
    nb = SCAN_BATCH
    assert T % tc == 0 and B % nb == 0
    rr = lax.broadcasted_iota(jnp.int32, (RW_QUAD, RW_QUAD), 0)
    cc = lax.broadcasted_iota(jnp.int32, (RW_QUAD, RW_QUAD), 1)
    tsel = jnp.where((rr // V_HALF) % 4 == cc // RW_HEAD, rr % V_HALF, -1)
    ones = _block_diag_ones(BF16)
    ones2 = jnp.concatenate([ones, ones], axis=0)
    xspec = pl.BlockSpec((nb, tc, D_RWKV), lambda bi, ci: (bi, ci, 0))
    sspec = pl.BlockSpec((nb, RW_HEADS, RW_HEAD, RW_HEAD), lambda bi, ci: (bi, 0, 0, 0))
    const = lambda shape: pl.BlockSpec(shape, lambda bi, ci: (0,) * len(shape))
    o, s_fin = pl.pallas_call(
        functools.partial(_scan_kernel, tc=tc, nb=nb),
        grid=(B // nb, nc),
        in_specs=[pl.BlockSpec((D_RWKV // LANES, nb, tc, LANES), lambda bi, ci: (0, bi, ci, 0))] * 5
                 + [xspec] * 3 + [sspec, const((RW_QUAD, RW_QUAD)),
                                const((RW_QUAD, RW_QUAD)), const((2 * RW_QUAD, RW_QUAD)),
                                const((1, D_RWKV)), const((1, D_RWKV))],
        out_specs=[xspec, sspec],
        out_shape=[jax.ShapeDtypeStruct((B, T, D_RWKV), BF16),
                   jax.ShapeDtypeStruct((B, RW_HEADS, RW_HEAD, RW_HEAD), F32)],
        scratch_shapes=[pltpu.VMEM((nb, RW_ROWS, RW_QUAD), F32),
                        pltpu.VMEM((nb * RW_ROWS, RW_QUAD), F32), pltpu.VMEM((tc, D_RWKV), F32),
                        pltpu.VMEM(((tc + V_HALF - 1) // V_HALF, nb * RW_ROWS, RW_QUAD), BF16),
                        pltpu.VMEM((RW_ROWS, LANES), F32)],
        compiler_params=_cparams(("parallel", "arbitrary")),
        name="rwkv_scan",
    )(w, r, k, kk, b, v, bonus, g, s0, tsel, ones, ones2,
      ln_w.reshape(1, D_RWKV), ln_b.reshape(1, D_RWKV))
    return o, s_fin


ATT_SCALE = ATT_QK ** -0.5


def _lam_from(lamp_ref):
    lp = lamp_ref[...]
    return (jnp.exp(jnp.sum(lp[0:1] * lp[1:2])) - jnp.exp(jnp.sum(lp[2:3] * lp[3:4])) + LAM_INIT)


def _online_softmax_step(s, v_bf, m, l, a):
    m_new = jnp.maximum(m, jnp.max(s, axis=-1, keepdims=True))
    alpha = jnp.exp(m - m_new)
    p = jnp.exp(s - m_new)
    l = alpha * l + jnp.sum(p, axis=-1, keepdims=True)
    a = alpha * a + jnp.dot(p.astype(BF16), v_bf, preferred_element_type=F32)
    return m_new, l, a


def _subln(o, g):
    ms = jnp.mean(o * o, axis=-1, keepdims=True)
    return o * lax.rsqrt(ms + NORM_EPS) * g * (1.0 - LAM_INIT)


def _diff_prompt_kernel(slopes_ref, q_ref, k_ref, v_ref, lamp_ref, g_ref, o_ref, *, tq, tk):
    h = pl.program_id(1)
    qi = pl.program_id(2)
    slope = slopes_ref[h]
    lam = _lam_from(lamp_ref)
    q = q_ref[...]
    q1 = q[:, :ATT_QK].astype(BF16)
    q2 = q[:, ATT_QK:].astype(BF16)
    rel = (lax.broadcasted_iota(jnp.int32, (tq, tk), 0) - lax.broadcasted_iota(jnp.int32, (tq, tk), 1))
    base = -slope * rel.astype(F32)

    def kv_step(kb, carry, diagonal=False):
        m1, l1, a1, m2, l2, a2 = carry
        k0 = pl.multiple_of(kb * tk, tk)
        k = k_ref[pl.ds(k0, tk), :]
        v = v_ref[pl.ds(k0, tk), :].astype(BF16)
        bias = base - slope * (qi * tq - k0).astype(F32)
        if diagonal:
            bias = jnp.where(rel >= 0, bias, NEG_INF)
        s1 = lax.dot_general(q1, k[:, :ATT_QK].astype(BF16), DOT_NT, preferred_element_type=F32) * ATT_SCALE + bias
        s2 = lax.dot_general(q2, k[:, ATT_QK:].astype(BF16), DOT_NT, preferred_element_type=F32) * ATT_SCALE + bias
        m1, l1, a1 = _online_softmax_step(s1, v, m1, l1, a1)
        m2, l2, a2 = _online_softmax_step(s2, v, m2, l2, a2)
        return m1, l1, a1, m2, l2, a2

    col = lambda val: jnp.full((tq, 1), val, F32)
    acc0 = jnp.zeros((tq, ATT_V), F32)
    carry = lax.fori_loop(0, qi, kv_step, (col(NEG_INF), col(0.0), acc0, col(NEG_INF), col(0.0), acc0))
    m1, l1, a1, m2, l2, a2 = kv_step(qi, carry, diagonal=True)
    o = a1 / l1 - lam * (a2 / l2)
    o_ref[...] = _subln(o, g_ref[...]).astype(o_ref.dtype)


def diff_attn_prompt(q, k, v, B, T, slopes, lamp, g_subln, *, tq=512, tk=512):
    assert tq == tk and T % tq == 0
    nq = T // tq
    return pl.pallas_call(
        functools.partial(_diff_prompt_kernel, tq=tq, tk=tk),
        grid=(B, ATT_HEADS, nq),
        in_specs=[pl.BlockSpec(memory_space=pltpu.SMEM),
                  pl.BlockSpec((tq, ATT_V), lambda b, h, i: (b * nq + i, h)),
                  pl.BlockSpec((T, ATT_V), lambda b, h, i: (b, h)),
                  pl.BlockSpec((T, ATT_V), lambda b, h, i: (b, h)),
                  pl.BlockSpec((4, ATT_QK), lambda b, h, i: (0, 0)),
                  pl.BlockSpec((1, ATT_V), lambda b, h, i: (0, 0))],
        out_specs=pl.BlockSpec((tq, ATT_V), lambda b, h, i: (b * nq + i, h)),
        out_shape=jax.ShapeDtypeStruct((B * T, D_ATT), BF16),
        compiler_params=_cparams(("parallel", "parallel", "arbitrary")),
        name="diff_attn_prompt",
    )(slopes, q, k, v, lamp, g_subln.reshape(1, ATT_V))


PAGES_PER_STEP = 8
QROWS = 16


PAGE_ROWS = PAGE_SIZE * ATT_HEADS
DOT_TN = (((0,), (0,)), ((), ()))


def _diff_sample_kernel(pt_ref, slopes_ref, q_ref, kn_ref, vn_ref, lamp_ref, g_ref, *rest, ds, past):
    k_refs = rest[:PAGES_PER_STEP]
    v_refs = rest[PAGES_PER_STEP:2 * PAGES_PER_STEP]
    o_ref = rest[2 * PAGES_PER_STEP]
    qw_ref, bias_ref, m_ref, l_ref, a_ref = rest[2 * PAGES_PER_STEP + 1:]
    j = pl.program_id(1)
    cph = 2 * ds
    col = lax.broadcasted_iota(jnp.int32, (1, LANES), 1)
    col_head = col // cph
    col_q = col % ds
    slope_col = jnp.zeros((1, LANES), F32)
    for h in range(ATT_HEADS):
        slope_col = jnp.where(col_head == h, slopes_ref[h], slope_col)
    row_key = lax.broadcasted_iota(jnp.int32, (PAGE_ROWS, 1), 0) // ATT_HEADS
    row_head = lax.broadcasted_iota(jnp.int32, (PAGE_ROWS, 1), 0) % ATT_HEADS

    @pl.when(j == 0)
    def _():
        lane = lax.broadcasted_iota(jnp.int32, (ds, ATT_V), 1)
        rows = []
        for h in range(ATT_HEADS):
            qh = q_ref[0, :, h * ATT_V:(h + 1) * ATT_V]
            rows += [jnp.where(lane < ATT_QK, qh, 0.0), jnp.where(lane >= ATT_QK, qh, 0.0)]
        rows.append(jnp.zeros((LANES - ATT_HEADS * cph, ATT_V), F32))
        qw_ref[...] = jnp.concatenate(rows, axis=0).T.astype(BF16)
        bias_ref[...] = jnp.where(row_head == col_head, slope_col * row_key.astype(F32), NEG_INF)
        m_ref[...] = jnp.full(m_ref.shape, NEG_INF, F32)
        l_ref[...] = jnp.zeros(l_ref.shape, F32)
        a_ref[...] = jnp.zeros(a_ref.shape, F32)

    eye = (lax.broadcasted_iota(jnp.int32, (LANES, LANES), 0) == lax.broadcasted_iota(jnp.int32, (LANES, LANES), 1))
    to_col = lambda r: jnp.sum(jnp.where(eye, r, 0.0), axis=1, keepdims=True)

    def attend(k_list, v_list, key0, extra):
        s_pages = []
        for n, k_ref in enumerate(k_list):
            k2 = k_ref[0].reshape(PAGE_ROWS, ATT_V).astype(BF16)
            s = jnp.dot(k2, qw_ref[...], preferred_element_type=F32) * ATT_SCALE + bias_ref[...]
            s_pages.append(s + slope_col * (key0 + n * PAGE_SIZE - past - col_q).astype(F32))
        s = jnp.concatenate(s_pages, axis=0)
        if extra is not None:
            s = jnp.where(extra, s, NEG_INF)
        v2 = jnp.concatenate([v_ref[0].reshape(PAGE_ROWS, ATT_V).astype(BF16) for v_ref in v_list], axis=0)
        m_old = m_ref[...]
        m_new = jnp.maximum(m_old, jnp.max(s, axis=0, keepdims=True))
        alpha = jnp.exp(m_old - m_new)
        p = jnp.exp(s - m_new)
        l_ref[...] = alpha * l_ref[...] + jnp.sum(p, axis=0, keepdims=True)
        m_ref[...] = m_new
        pv = lax.dot_general(p.astype(BF16), v2, DOT_TN, preferred_element_type=F32)
        a_ref[...] = a_ref[...] * to_col(alpha) + pv

    attend(k_refs, v_refs, j * PAGES_PER_STEP * PAGE_SIZE, None)

    @pl.when(j == pl.num_programs(1) - 1)
    def _():
        attend([kn_ref], [vn_ref], past, (row_key < ds) & (row_key <= col_q))
        lam = _lam_from(lamp_ref)
        o_all = a_ref[...] / to_col(l_ref[...])
        for h in range(ATT_HEADS):
            o = o_all[h * cph:h * cph + ds] - lam * o_all[h * cph + ds:(h + 1) * cph]
            o_ref[0, :, h * ATT_V:(h + 1) * ATT_V] = _subln(o, g_ref[...]).astype(o_ref.dtype)


def diff_attn_sample(q, k_new, v_new, cache_k, cache_v, page_table, slopes, lamp, g_subln):
    B, ds, _ = q.shape
    n_pages = page_table.shape[1]
    past = n_pages * PAGE_SIZE
    assert n_pages % PAGES_PER_STEP == 0 and ATT_HEADS * 2 * ds <= LANES
    pad = lambda t: jnp.pad(t, ((0, 0), (0, PAGE_SIZE - ds), (0, 0), (0, 0)))
    page_blk = (1, PAGE_SIZE, ATT_HEADS, ATT_V)

    def page_spec(i):
        return pl.BlockSpec(page_blk, lambda b, j, pt: (pt[b, j * PAGES_PER_STEP + i], 0, 0, 0))

    fixed = lambda shape: pl.BlockSpec(shape, lambda b, j, pt: (0,) * len(shape))
    grid_spec = pltpu.PrefetchScalarGridSpec(
        num_scalar_prefetch=1,
        grid=(B, n_pages // PAGES_PER_STEP),
        in_specs=[pl.BlockSpec(memory_space=pltpu.SMEM),
                  pl.BlockSpec((1, ds, D_ATT), lambda b, j, pt: (b, 0, 0)),
                  pl.BlockSpec(page_blk, lambda b, j, pt: (b, 0, 0, 0)),
                  pl.BlockSpec(page_blk, lambda b, j, pt: (b, 0, 0, 0)),
                  fixed((4, ATT_QK)), fixed((1, ATT_V))]
                 + [page_spec(i) for i in range(PAGES_PER_STEP)] * 2,
        out_specs=pl.BlockSpec((1, ds, D_ATT), lambda b, j, pt: (b, 0, 0)),
        scratch_shapes=[pltpu.VMEM((ATT_V, LANES), BF16),
                        pltpu.VMEM((PAGE_ROWS, LANES), F32),
                        pltpu.VMEM((1, LANES), F32),
                        pltpu.VMEM((1, LANES), F32),
                        pltpu.VMEM((LANES, ATT_V), F32)])
    return pl.pallas_call(
        functools.partial(_diff_sample_kernel, ds=ds, past=past),
        grid_spec=grid_spec,
        out_shape=jax.ShapeDtypeStruct((B, ds, D_ATT), BF16),
        compiler_params=_cparams(("parallel", "arbitrary")),
        name="diff_attn_sample",
    )(page_table, slopes, q, pad(k_new), pad(v_new), lamp, g_subln.reshape(1, ATT_V),
      *([cache_k] * PAGES_PER_STEP), *([cache_v] * PAGES_PER_STEP))


def _mem_attn_kernel(q_ref, k_ref, v_ref, o_ref, *, paged_heads, rows):
    for h in range(MEM_HEADS):
        sl = slice(h * MEM_HEAD, (h + 1) * MEM_HEAD)
        q = q_ref[:, sl] if q_ref.ndim == 2 else q_ref[0, :, sl]
        if q.shape[0] < rows:
            q = jnp.concatenate([q, jnp.zeros((rows - q.shape[0], MEM_HEAD), F32)], axis=0)
        k = k_ref[0, :, h, :] if paged_heads else k_ref[:, sl]
        v = v_ref[0, :, h, :] if paged_heads else v_ref[:, sl]
        s = lax.dot_general(q.astype(BF16), k.astype(BF16), DOT_NT, preferred_element_type=F32) * (MEM_HEAD ** -0.5)
        p = jnp.exp(s - jnp.max(s, axis=-1, keepdims=True))
        o = jnp.dot(p.astype(BF16), v.astype(BF16), preferred_element_type=F32) / jnp.sum(p, axis=-1, keepdims=True)
        if q_ref.ndim == 2:
            o_ref[:, sl] = o.astype(o_ref.dtype)
        else:
            o_ref[0, :, sl] = o[:q_ref.shape[1]].astype(o_ref.dtype)


def mem_attn_prompt(p_c, kv, B, T, MT, *, tq=512):
    tq = min(tq, T)
    nq = T // tq
    return pl.pallas_call(
        functools.partial(_mem_attn_kernel, paged_heads=False, rows=tq),
        grid=(B, nq),
        in_specs=[pl.BlockSpec((tq, D_MEM), lambda b, i: (b * nq + i, 0)),
                  pl.BlockSpec((MT, D_MEM), lambda b, i: (b, 0)),
                  pl.BlockSpec((MT, D_MEM), lambda b, i: (b, 1))],
        out_specs=pl.BlockSpec((tq, D_MEM), lambda b, i: (b * nq + i, 0)),
        out_shape=jax.ShapeDtypeStruct((B * T, D_MEM), BF16),
        compiler_params=_cparams(("parallel", "parallel")),
        name="mem_attn_prompt",
    )(p_c, kv, kv)


def mem_attn_sample(qm, mem_k, mem_v):
    B, ds, _ = qm.shape
    MT = mem_k.shape[1]
    kv_spec = pl.BlockSpec((1, MT, MEM_HEADS, MEM_HEAD), lambda b: (b, 0, 0, 0))
    return pl.pallas_call(
        functools.partial(_mem_attn_kernel, paged_heads=True, rows=QROWS),
        grid=(B,),
        in_specs=[pl.BlockSpec((1, ds, D_MEM), lambda b: (b, 0, 0)), kv_spec, kv_spec],
        out_specs=pl.BlockSpec((1, ds, D_MEM), lambda b: (b, 0, 0)),
        out_shape=jax.ShapeDtypeStruct((B, ds, D_MEM), BF16),
        compiler_params=_cparams(("parallel",)),
        name="mem_attn_sample",
    )(qm, mem_k, mem_v)


PRE_ROWS = 128


def _rwkv7_branch(p_a, B, T, shift_prev, s0, W):
    M = B * T
    sp = jnp.pad(shift_prev, ((0, 0), (0, RW_COLS_PAD - RW_COLS)))
    if T % PRE_ROWS == 0:
        first = jnp.arange(M // PRE_ROWS) * PRE_ROWS
        rows = jnp.where((first % T == 0)[:, None], sp[first // T], p_a[jnp.maximum(first - 1, 0)])
        prev = jnp.zeros((M // PRE_ROWS, 8, RW_COLS_PAD), F32).at[:, 0].set(rows).reshape(-1, RW_COLS_PAD)
        boundary = True
    else:
        p3 = p_a.reshape(B, T, RW_COLS_PAD)
        prev = jnp.concatenate([sp[:, None], p3[:, :-1]], axis=1).reshape(M, RW_COLS_PAD)
        boundary = False
    outs = rwkv_pre(p_a, prev, boundary, T, W, tr=PRE_ROWS)
    per_step = [t.reshape(D_RWKV // LANES, B, T, LANES) for t in outs[:5]]
    v, bonus, g = (t.reshape(B, T, D_RWKV) for t in outs[5:])
    o, s_fin = rwkv_scan(*per_step, v, bonus, g, s0, W['ln_x_w'], W['ln_x_b'])
    shift_last = p_a.reshape(B, T, RW_COLS_PAD)[:, -1, :RW_COLS]
    return o.reshape(M, D_RWKV), s_fin, shift_last


def _layer(x, shift_prev, s0, attend, mem_attend, W):
    B, T, _ = x.shape
    M = B * T
    x2 = x.reshape(M, D_MODEL)
    h = rmsnorm_bf16(x2, W['g_pre_mix'])
    wt = W['w_in_t']
    seg_a = (0, RW_COLS_PAD)
    seg_qkv = [(RW_COLS + g * D_ATT, D_ATT) for g in range(3)]
    seg_c = (RW_COLS + 3 * D_ATT, D_MEM + 3 * D_MODEL)
    if M <= MULTI_TM:
        p_a, q, k, v, p_c = matmul_nt_multi(h, wt, [seg_a] + seg_qkv + [seg_c], tm=MULTI_TM)
    else:
        p_a = matmul_nt(h, wt, row0=seg_a[0], n_cols=seg_a[1])
        q, k, v = matmul_nt_multi(h, wt, seg_qkv, tm=MULTI_TM)
        p_c = matmul_nt(h, wt, row0=seg_c[0], n_cols=seg_c[1])

    o_rw, s_fin, shift_last = _rwkv7_branch(p_a, B, T, shift_prev, s0, W)
    o_att = attend(q, k, v)
    o_mem = mem_attend(p_c)

    m = gated_merge(o_rw, o_att, o_mem,
                    W['w_br_rwkv'], W['w_br_attn'], W['w_br_mem'], p_c, D_MEM)
    k = k.reshape(B, T, ATT_HEADS, 2 * ATT_QK)
    v = v.reshape(B, T, ATT_HEADS, ATT_V)
    z = matmul(m, W['w_out'])
    x1, h2 = norm_residual(z, W['g_post_mix'], x2, W['g_pre_mlp'])
    u = matmul(h2, W['w_up'], out_dtype=BF16, relu2=True)
    z2 = matmul(u, W['w_down'], tm=2048, tk=2048)
    y = norm_residual(z2, W['g_post_mlp'], x1)
    return y.reshape(B, T, D_MODEL), k, v, s_fin, shift_last


def kernel(x_prompt, x_sample, cache_k, cache_v, page_table, state_wkv, state_shift, cache_mem_k, cache_mem_v, mem_prompt, g_pre_mix, w_in, mu_shift, w0, w_lora_up, a0, a_lora_up, g_lora_up, k_k, k_a, r_k, ln_x_w, ln_x_b, lambda_q1, lambda_k1, lambda_q2, lambda_k2, g_subln, g_mem, w_mem_kv, w_br_rwkv, w_br_attn, w_br_mem, w_out, g_post_mix, g_pre_mlp, w_up, w_down, g_post_mlp):
    W = dict(g_pre_mix=g_pre_mix, w_in_t=w_in.T,
             mu_shift=mu_shift, w0=w0, w_lora_up=w_lora_up,
             a0=a0, a_lora_up=a_lora_up, g_lora_up=g_lora_up, k_k=k_k, k_a=k_a, r_k=r_k,
             ln_x_w=ln_x_w, ln_x_b=ln_x_b, lambda_q1=lambda_q1, lambda_k1=lambda_k1,
             lambda_q2=lambda_q2, lambda_k2=lambda_k2, g_subln=g_subln, w_br_rwkv=w_br_rwkv.astype(BF16),
             w_br_attn=w_br_attn.astype(BF16), w_br_mem=w_br_mem.astype(BF16), w_out=w_out,
             g_post_mix=g_post_mix,
             g_pre_mlp=g_pre_mlp, w_up=w_up, w_down=w_down, g_post_mlp=g_post_mlp)
    slopes = jnp.asarray([2.0 ** (-8.0 * (h + 1) / ATT_HEADS) for h in range(ATT_HEADS)], dtype=F32)
    lamp = jnp.stack([lambda_q1, lambda_k1, lambda_q2, lambda_k2])
    B, T, _ = x_prompt.shape
    BS, DS, _ = x_sample.shape
    MT = mem_prompt.shape[1]

    hm = rmsnorm_bf16(mem_prompt.reshape(B * MT, D_MODEL), g_mem)
    kv = matmul(hm, w_mem_kv)
    mem_k_prompt = kv[:, :D_MEM].reshape(B, MT, MEM_HEADS, MEM_HEAD)
    mem_v_prompt = kv[:, D_MEM:].reshape(B, MT, MEM_HEADS, MEM_HEAD)

    y_prompt, k_prompt, v_prompt, wkv_prompt, shift_prompt = _layer(
        x_prompt, jnp.zeros((B, RW_COLS), F32), jnp.zeros((B, RW_HEADS, RW_HEAD, RW_HEAD), F32),
        lambda q, k, v: diff_attn_prompt(q, k, v, B, T, slopes, lamp, g_subln),
        lambda p_c: mem_attn_prompt(p_c, kv, B, T, MT), W)

    def attend_sample(q, k, v):
        page = lambda t: t.reshape(BS, DS, ATT_HEADS, ATT_V)
        o = diff_attn_sample(q.reshape(BS, DS, D_ATT), page(k), page(v), cache_k, cache_v, page_table,
                             slopes, lamp, g_subln)
        return o.reshape(BS * DS, D_ATT)

    def mem_attend_sample(p_c):
        qm = p_c[:, :D_MEM].reshape(BS, DS, D_MEM)
        return mem_attn_sample(qm, cache_mem_k, cache_mem_v).reshape(BS * DS, D_MEM)

    y_sample, k_sample, v_sample, wkv_sample, shift_sample = _layer(
        x_sample, state_shift, state_wkv, attend_sample, mem_attend_sample, W)
    return (y_prompt, y_sample, k_prompt, v_prompt, k_sample, v_sample,
            wkv_prompt, wkv_sample, shift_prompt, shift_sample, mem_k_prompt, mem_v_prompt)
```
